```python
import jax, jax.numpy as jnp
from jax import lax
import numpy as np

D_MODEL = 1024
BATCH = 2
SEQ = 8192
DEPTH = 4

NSA_HEADS = 8
NSA_KV_HEADS = 2
NSA_HEAD_DIM = 64
NSA_GROUP = NSA_HEADS // NSA_KV_HEADS
NSA_WIDTH = NSA_HEADS * NSA_HEAD_DIM
CMP_BLOCK = 32
CMP_STRIDE = 16
SLC_BLOCK = 64
SLC_TOPK = 16
WINDOW = 512
Q_BLOCK = 128
FORCE_SCORE = 1e4
NEG_INF = -1e30
RWKV_HEADS = 8
RWKV_HEAD_DIM = 64
RWKV_WIDTH = RWKV_HEADS * RWKV_HEAD_DIM
DECAY_RANK = 64
ICLR_RANK = 64
VRES_RANK = 32
GATE_RANK = 128
LNX_EPS = 1e-5 * RWKV_HEAD_DIM
D_FF = 2816
CONV_WIDTH = 3
EPS = 1e-6
NSA_Q_COLS = NSA_WIDTH
NSA_KV_COLS = NSA_KV_HEADS * NSA_HEAD_DIM
NSA_GATE_COLS = NSA_HEADS * 3
RWKV_SPLITS = (RWKV_WIDTH, RWKV_WIDTH, RWKV_WIDTH, DECAY_RANK, ICLR_RANK, GATE_RANK)
RWKV_COLS = sum(RWKV_SPLITS)
MERGE_COLS = 2 * D_MODEL
IN_SPLITS = (NSA_Q_COLS, NSA_KV_COLS, NSA_KV_COLS, NSA_KV_COLS, NSA_KV_COLS, NSA_KV_COLS, NSA_KV_COLS, NSA_GATE_COLS, RWKV_COLS, MERGE_COLS)
IN_COLS = sum(IN_SPLITS)

kernel_name = 'nsa_rwkv7_gated_hybrid'


def _split(z, sizes):
    return jnp.split(z, np.cumsum(sizes)[:-1].tolist(), axis=-1)


def _rms_norm(x, g):
    x32 = x.astype(jnp.float32)
    y = x32 * lax.rsqrt(jnp.mean(x32 * x32, axis=-1, keepdims=True) + EPS)
    return (y * g.astype(jnp.float32)).astype(x.dtype)


def _masked_softmax(s, mask):
    p = jax.nn.softmax(jnp.where(mask, s.astype(jnp.float32), NEG_INF), axis=-1) * mask
    return p.astype(s.dtype)


def _nsa(q, kc, vc, ks, vs, kw, vw, gate_logits, qk_gain, cmp_pe, cmp_w1, cmp_w2):
    B, S, _ = q.shape
    dt = q.dtype
    G, HG, Dh = NSA_KV_HEADS, NSA_GROUP, NSA_HEAD_DIM
    scale = Dh ** -0.5
    qh = _rms_norm(q.reshape(B, S, G, HG, Dh), qk_gain[0]).transpose(0, 2, 3, 1, 4)

    nc = (S - CMP_BLOCK) // CMP_STRIDE + 1
    tok = np.arange(nc)[:, None] * CMP_STRIDE + np.arange(CMP_BLOCK)[None, :]

    def compress(z, pe, w1, w2):
        blocks = z.reshape(B, S, G, Dh)[:, tok] + pe[None, None, :, None, :]
        hid = jax.nn.silu(jnp.einsum('bnlgd,lde->bnge', blocks, w1))
        return jnp.einsum('bnge,ef->bgnf', hid, w2)

    k_cmp = _rms_norm(compress(kc, cmp_pe[0], cmp_w1[0], cmp_w2[0]), qk_gain[1])
    v_cmp = compress(vc, cmp_pe[1], cmp_w1[1], cmp_w2[1])
    cmp_end = jnp.arange(nc) * CMP_STRIDE + CMP_BLOCK - 1

    ns = S // SLC_BLOCK
    n_sel = min(SLC_TOPK, ns)
    k_slc = _rms_norm(ks.reshape(B, S, G, Dh), qk_gain[2]).transpose(0, 2, 1, 3).reshape(B, G, ns, SLC_BLOCK, Dh)
    v_slc = vs.reshape(B, S, G, Dh).transpose(0, 2, 1, 3).reshape(B, G, ns, SLC_BLOCK, Dh)
    n_idx = jnp.arange(nc)[:, None]
    s_idx = jnp.arange(ns)
    overlap = ((n_idx * CMP_STRIDE < (s_idx[None, :] + 1) * SLC_BLOCK)
               & (n_idx * CMP_STRIDE + CMP_BLOCK > s_idx[None, :] * SLC_BLOCK)).astype(jnp.float32)

    pad = ((0, 0), (0, 0), (WINDOW, 0), (0, 0))
    k_win = jnp.pad(_rms_norm(kw.reshape(B, S, G, Dh), qk_gain[3]).transpose(0, 2, 1, 3), pad)
    v_win = jnp.pad(vw.reshape(B, S, G, Dh).transpose(0, 2, 1, 3), pad)

    gates = jax.nn.sigmoid(gate_logits.astype(jnp.float32)).astype(dt)
    gates = gates.reshape(B, S, G, HG, 3).transpose(0, 2, 3, 1, 4)
    bi = jnp.arange(B)[:, None, None, None]
    gi = jnp.arange(G)[None, :, None, None]

    def block(c):
        t0 = c * Q_BLOCK
        t = t0 + jnp.arange(Q_BLOCK)
        qb = lax.dynamic_slice_in_dim(qh, t0, Q_BLOCK, axis=3)
        gb = lax.dynamic_slice_in_dim(gates, t0, Q_BLOCK, axis=3)
        s = jnp.einsum('bghtd,bgnd->bghtn', qb, k_cmp) * scale
        p_cmp = _masked_softmax(s, cmp_end[None, :] <= t[:, None])
        o_cmp = jnp.einsum('bghtn,bgnd->bghtd', p_cmp, v_cmp)
        imp = jnp.einsum('bghtn,ns->bgts', p_cmp.astype(jnp.float32), overlap)
        cur = t // SLC_BLOCK
        forced = (s_idx[None, :] == 0) | (s_idx[None, :] == cur[:, None]) | (s_idx[None, :] == cur[:, None] - 1)
        imp = jnp.where(forced, FORCE_SCORE, imp)
        imp = jnp.where(s_idx[None, :] > cur[:, None], -1.0, imp)
        _, sel = lax.top_k(imp, n_sel)
        k_sel = k_slc[bi, gi, sel].reshape(B, G, Q_BLOCK, n_sel * SLC_BLOCK, Dh)
        v_sel = v_slc[bi, gi, sel].reshape(B, G, Q_BLOCK, n_sel * SLC_BLOCK, Dh)
        pos = (sel[..., None] * SLC_BLOCK + jnp.arange(SLC_BLOCK)).reshape(B, G, Q_BLOCK, n_sel * SLC_BLOCK)
        s = jnp.einsum('bghtd,bgtkd->bghtk', qb, k_sel) * scale
        p_slc = _masked_softmax(s, pos[:, :, None] <= t[:, None])
        o_slc = jnp.einsum('bghtk,bgtkd->bghtd', p_slc, v_sel)
        k_w = lax.dynamic_slice_in_dim(k_win, t0, WINDOW + Q_BLOCK, axis=2)
        v_w = lax.dynamic_slice_in_dim(v_win, t0, WINDOW + Q_BLOCK, axis=2)
        kpos = t0 - WINDOW + jnp.arange(WINDOW + Q_BLOCK)
        dist = t[:, None] - kpos[None, :]
        s = jnp.einsum('bghtd,bgkd->bghtk', qb, k_w) * scale
        p_win = _masked_softmax(s, (kpos[None, :] >= 0) & (dist >= 0) & (dist < WINDOW))
        o_win = jnp.einsum('bghtk,bgkd->bghtd', p_win, v_w)
        return gb[..., 0:1] * o_cmp + gb[..., 1:2] * o_slc + gb[..., 2:3] * o_win

    o = lax.map(block, jnp.arange(S // Q_BLOCK))
    return o.transpose(1, 0, 4, 2, 3, 5).reshape(B, S, NSA_WIDTH)


def _wkv7_scan(r, w, k, v, a, b):
    B, S, H, N = r.shape

    def step(state, inp):
        r_t, w_t, k_t, v_t, a_t, b_t = inp
        sa = jnp.einsum('bhvk,bhk->bhv', state, a_t)
        state = state * w_t[:, :, None, :] + sa[..., None] * b_t[:, :, None, :] + v_t[..., None] * k_t[:, :, None, :]
        return state, jnp.einsum('bhvk,bhk->bhv', state, r_t)

    xs = (jnp.swapaxes(r, 0, 1), jnp.swapaxes(w, 0, 1), jnp.swapaxes(k, 0, 1),
          jnp.swapaxes(v, 0, 1), jnp.swapaxes(a, 0, 1), jnp.swapaxes(b, 0, 1))
    _, y = lax.scan(step, jnp.zeros((B, H, N, N), jnp.float32), xs)
    return jnp.swapaxes(y, 0, 1)


def _rwkv7(z, v_first, vres, mu, w0, w_up, a0, a_up, g_up, k_k, k_a, r_k, ln_w, ln_b):
    B, S, _ = z.shape
    dt = z.dtype
    f32 = jnp.float32
    H, N = RWKV_HEADS, RWKV_HEAD_DIM
    z_prev = jnp.pad(z[:, :-1], ((0, 0), (1, 0), (0, 0)))
    z = z + mu * (z_prev - z)
    r, k, v, wd, ad, gd = _split(z, RWKV_SPLITS)
    if vres is None:
        v_first = v
    else:
        v0, v1, v2 = vres
        v = v + (v_first - v) * jax.nn.sigmoid(v0 + (v @ v1) @ v2)
    w = -jax.nn.softplus(-(w0 + jnp.tanh(wd) @ w_up).astype(f32)) - 0.5
    decay = jnp.exp(-jnp.exp(w))
    a = jax.nn.sigmoid((a0 + ad @ a_up).astype(f32))
    g = jax.nn.sigmoid(gd) @ g_up

    def heads(t):
        return t.astype(f32).reshape(B, S, H, N)

    rh, vh, ah, wh = heads(r), heads(v), heads(a), heads(decay)
    kk = heads(k * k_k)
    kk = kk / jnp.maximum(jnp.sqrt(jnp.sum(kk * kk, axis=-1, keepdims=True)), 1e-12)
    kh = heads(k) * (1.0 + (ah - 1.0) * k_a.astype(f32).reshape(H, N))
    y = _wkv7_scan(rh, wh, kh, vh, -kk, kk * ah)
    mean = jnp.mean(y, axis=-1, keepdims=True)
    var = jnp.mean(jnp.square(y - mean), axis=-1, keepdims=True)
    y = (y - mean) * lax.rsqrt(var + LNX_EPS) * ln_w.astype(f32).reshape(H, N) + ln_b.astype(f32).reshape(H, N)
    y = y + jnp.sum(rh * kh * r_k.astype(f32), axis=-1, keepdims=True) * vh
    return y.reshape(B, S, RWKV_WIDTH).astype(dt) * g, v_first


def _conv_ffn(h, w_up, conv_w, w_down):
    u = h @ w_up
    u = lax.conv_general_dilated(u, conv_w[:, None, :], window_strides=(1,),
                                 padding=((CONV_WIDTH - 1, 0),),
                                 dimension_numbers=('NWC', 'WIO', 'NWC'),
                                 feature_group_count=u.shape[-1])
    a, b = jnp.split(u, 2, axis=-1)
    return (jax.nn.silu(a) * b) @ w_down


def setup_inputs(seed: int = 0) -> dict:
    key = jax.random.key(seed)
    k = jax.random.split(key, 32)
    L, D, Dh = DEPTH, D_MODEL, NSA_HEAD_DIM
    f32 = jnp.float32

    def nrm(kk, shape, scale):
        return jax.random.normal(kk, shape, f32) * scale

    def gain(kk, shape):
        return 1.0 + 0.1 * jax.random.normal(kk, shape, f32)

    return {
        'x': nrm(k[0], (BATCH, SEQ, D), 1.0),
        'norm_mix': gain(k[1], (L, D)),
        'norm_ffn': gain(k[2], (L, D)),
        'w_in': nrm(k[3], (L, D, IN_COLS), D ** -0.5),
        'qk_gain': gain(k[4], (L, 4, Dh)),
        'cmp_pe': nrm(k[5], (L, 2, CMP_BLOCK, Dh), 0.1),
        'cmp_w1': nrm(k[6], (L, 2, CMP_BLOCK, Dh, Dh), (CMP_BLOCK * Dh) ** -0.5),
        'cmp_w2': nrm(k[7], (L, 2, Dh, Dh), Dh ** -0.5),
        'rwkv_mu': jax.random.uniform(k[8], (L, RWKV_COLS), f32),
        'rwkv_w0': jax.random.uniform(k[9], (L, RWKV_WIDTH), f32, -3.0, 2.0),
        'rwkv_w_up': nrm(k[10], (L, DECAY_RANK, RWKV_WIDTH), 0.1),
        'rwkv_a0': nrm(k[11], (L, RWKV_WIDTH), 0.5),
        'rwkv_a_up': nrm(k[12], (L, ICLR_RANK, RWKV_WIDTH), ICLR_RANK ** -0.5),
        'rwkv_g_up': nrm(k[13], (L, GATE_RANK, RWKV_WIDTH), GATE_RANK ** -0.5),
        'rwkv_k_k': 0.85 + 0.1 * jax.random.normal(k[14], (L, RWKV_WIDTH), f32),
        'rwkv_k_a': gain(k[15], (L, RWKV_WIDTH)),
        'rwkv_r_k': nrm(k[16], (L, RWKV_HEADS, RWKV_HEAD_DIM), 0.1),
        'rwkv_ln_w': gain(k[17], (L, RWKV_WIDTH)),
        'rwkv_ln_b': nrm(k[18], (L, RWKV_WIDTH), 0.02),
        'vres_v0': nrm(k[19], (L - 1, RWKV_WIDTH), 0.5),
        'vres_v1': nrm(k[20], (L - 1, RWKV_WIDTH, VRES_RANK), RWKV_WIDTH ** -0.5),
        'vres_v2': nrm(k[21], (L - 1, VRES_RANK, RWKV_WIDTH), VRES_RANK ** -0.5),
        'proj_nsa': nrm(k[22], (L, NSA_WIDTH, D), NSA_WIDTH ** -0.5),
        'proj_rwkv': nrm(k[23], (L, RWKV_WIDTH, D), RWKV_WIDTH ** -0.5),
        'w_out': nrm(k[24], (L, D, D), D ** -0.5),
        'ffn_up': nrm(k[25], (L, D, 2 * D_FF), D ** -0.5),
        'ffn_conv': nrm(k[26], (L, CONV_WIDTH, 2 * D_FF), CONV_WIDTH ** -0.5),
        'ffn_down': nrm(k[27], (L, D_FF, D), D_FF ** -0.5),
    }


def reference(x, norm_mix, norm_ffn, w_in, qk_gain, cmp_pe, cmp_w1, cmp_w2, rwkv_mu, rwkv_w0,
              rwkv_w_up, rwkv_a0, rwkv_a_up, rwkv_g_up, rwkv_k_k, rwkv_k_a, rwkv_r_k, rwkv_ln_w,
              rwkv_ln_b, vres_v0, vres_v1, vres_v2, proj_nsa, proj_rwkv, w_out, ffn_up, ffn_conv,
              ffn_down):
    B, S, D = x.shape
    v_first = None
    for l in range(DEPTH):
        h = _rms_norm(x, norm_mix[l])
        u = h @ w_in[l]
        q, kc, vc, ks, vs, kw, vw, nsa_gl, rw, merge = _split(u, IN_SPLITS)
        o_nsa = _nsa(q, kc, vc, ks, vs, kw, vw, nsa_gl, qk_gain[l], cmp_pe[l], cmp_w1[l], cmp_w2[l])
        vres = None if l == 0 else (vres_v0[l - 1], vres_v1[l - 1], vres_v2[l - 1])
        o_rwkv, v_first = _rwkv7(rw, v_first, vres, rwkv_mu[l], rwkv_w0[l], rwkv_w_up[l], rwkv_a0[l],
                                 rwkv_a_up[l], rwkv_g_up[l], rwkv_k_k[l], rwkv_k_a[l], rwkv_r_k[l],
                                 rwkv_ln_w[l], rwkv_ln_b[l])
        gates = jax.nn.sigmoid(merge.astype(jnp.float32)).astype(x.dtype).reshape(B, S, 2, D)
        y = gates[:, :, 0] * (o_nsa @ proj_nsa[l]) + gates[:, :, 1] * (o_rwkv @ proj_rwkv[l])
        x = x + y @ w_out[l]
        x = x + _conv_ffn(_rms_norm(x, norm_ffn[l]), ffn_up[l], ffn_conv[l], ffn_down[l])
    return x
```

```python
import functools

import jax
import jax.numpy as jnp
from jax import lax
from jax.experimental import pallas as pl
from jax.experimental.pallas import tpu as pltpu

F32 = jnp.float32
BF16 = jnp.bfloat16
HIGHEST = lax.Precision.HIGHEST

LANES = 128
VMEM_LIMIT_BYTES = 56 * 1024 * 1024

NSA_HEADS = 8
NSA_KV_HEADS = 2
NSA_GROUP = NSA_HEADS // NSA_KV_HEADS
HEAD_DIM = 64
CMP_BLOCK = 32
CMP_STRIDE = 16
SLC_BLOCK = 64
SLC_SHIFT = 6
SLC_TOPK = 16
WINDOW = 512
FORCE_SCORE = 1e4
NEG_INF = -1e30
RWKV_HEADS = 8
DECAY_RANK = 64
ICLR_RANK = 64
GATE_RANK = 128
VRES_RANK = 32
LNX_EPS = 1e-5 * HEAD_DIM
D_FF = 2816
EPS = 1e-6

NSA_W = NSA_HEADS * HEAD_DIM
KV_W = NSA_KV_HEADS * HEAD_DIM
RW_W = RWKV_HEADS * HEAD_DIM
RW_COLS = 3 * RW_W + DECAY_RANK + ICLR_RANK + GATE_RANK
NSA_SLAB = NSA_W + 6 * KV_W + NSA_KV_HEADS * LANES

Q_TILE = 128
SLC_KV_TILE = 512
RW_CHUNK = 64
RW_SUB = 16


def _cparams(*sem):
    return pltpu.CompilerParams(dimension_semantics=sem, vmem_limit_bytes=VMEM_LIMIT_BYTES)


def _dot(a, b, precision=None):
    return lax.dot_general(a, b, (((1,), (0,)), ((), ())), precision=precision,
                           preferred_element_type=F32)


def _dot_nt(a, b, precision=None):
    return lax.dot_general(a, b, (((1,), (1,)), ((), ())), precision=precision,
                           preferred_element_type=F32)


def _dot_tn(a, b, precision=None):
    return lax.dot_general(a, b, (((0,), (0,)), ((), ())), precision=precision,
                           preferred_element_type=F32)


def _rms(x, gain):
    return x * lax.rsqrt(jnp.mean(x * x, axis=-1, keepdims=True) + EPS) * gain


def _sigmoid(x):
    return 1.0 / (1.0 + jnp.exp(-x))


def _silu(x):
    return x * _sigmoid(x)


def _inproj_kernel(x_ref, g_ref, wn_ref, wr_ref, wm_ref, on_ref, or_ref, om_ref):
    h = _rms(x_ref[...], g_ref[...]).astype(BF16)
    on_ref[...] = _dot(h, wn_ref[...])
    or_ref[...] = _dot(h, wr_ref[...])
    om_ref[...] = _dot(h, wm_ref[...])


def _inproj(x, gain, w_nsa, w_rw, w_mg, tm=256):
    m, d = x.shape
    outs = [w_nsa.shape[1], w_rw.shape[1], w_mg.shape[1]]
    full = lambda i: (0, 0)
    return pl.pallas_call(
        _inproj_kernel,
        grid=(m // tm,),
        in_specs=[pl.BlockSpec((tm, d), lambda i: (i, 0)), pl.BlockSpec((1, d), full)]
        + [pl.BlockSpec((d, n), full) for n in outs],
        out_specs=[pl.BlockSpec((tm, n), lambda i: (i, 0)) for n in outs],
        out_shape=[jax.ShapeDtypeStruct((m, n), F32) for n in outs],
        compiler_params=_cparams("parallel"),
        name="inproj",
    )(x, gain, w_nsa, w_rw, w_mg)


def _nsa_prep_kernel(kc_ref, vc_ref, ks_ref, vs_ref, kw_ref, vw_ref, gn_ref, cmp_ref, ksn_ref,
                     vs_o_ref, kwn_ref, vw_o_ref):
    gn = gn_ref[...]
    for g in range(NSA_KV_HEADS):
        sl = slice(g * HEAD_DIM, (g + 1) * HEAD_DIM)
        cmp_ref[0, g] = kc_ref[:, sl]
        cmp_ref[1, g] = vc_ref[:, sl]
        ksn_ref[g] = _rms(ks_ref[:, sl], gn[2:3]).astype(BF16)
        kwn_ref[g] = _rms(kw_ref[:, sl], gn[3:4]).astype(BF16)
        vs_o_ref[g] = vs_ref[:, sl].astype(BF16)
        vw_o_ref[g] = vw_ref[:, sl].astype(BF16)


def _nsa_prep(u_nsa, qk_gain, tm=512):
    m = u_nsa.shape[0]
    g = NSA_KV_HEADS
    col0 = NSA_W // KV_W
    in_specs = [pl.BlockSpec((tm, KV_W), functools.partial(lambda i, c: (i, c), c=col0 + c))
                for c in range(6)]
    in_specs.append(pl.BlockSpec((4, HEAD_DIM), lambda i: (0, 0)))
    kv_spec = pl.BlockSpec((g, tm, HEAD_DIM), lambda i: (0, i, 0))
    kv_shape = jax.ShapeDtypeStruct((g, m, HEAD_DIM), BF16)
    return pl.pallas_call(
        _nsa_prep_kernel,
        grid=(m // tm,),
        in_specs=in_specs,
        out_specs=[pl.BlockSpec((2, g, tm, HEAD_DIM), lambda i: (0, 0, i, 0)), kv_spec, kv_spec,
                   kv_spec, kv_spec],
        out_shape=[jax.ShapeDtypeStruct((2, g, m, HEAD_DIM), F32), kv_shape, kv_shape, kv_shape,
                   kv_shape],
        compiler_params=_cparams("parallel"),
        name="nsa_prep",
    )(*([u_nsa] * 6), qk_gain)


def _nsa_compress_kernel(z_ref, w1ab_ref, pe_ref, w1_ref, w2_ref, gn_ref, o_ref):
    kv = pl.program_id(0)
    nhalf = z_ref.shape[0]
    p = _dot(z_ref[...].astype(BF16), w1ab_ref[...])
    a = p[:, :HEAD_DIM]
    b_next = pltpu.roll(p[:, HEAD_DIM:], nhalf - 1, axis=0)
    c = _dot(pe_ref[...].astype(BF16), w1_ref[...])[0:1]
    hid = _silu(a + b_next + c)
    o = _dot(hid.astype(BF16), w2_ref[...])
    o_ref[...] = jnp.where(kv == 0, _rms(o, gn_ref[1:2]), o)


def _nsa_compress(z, w1ab, pe8, w1flat, w2, qk_gain):
    _, g, b, nhalf, zw = z.shape
    sq = lambda *shape: pl.BlockSpec((None,) + shape, lambda kv, gi, bi: (kv,) + (0,) * len(shape))
    return pl.pallas_call(
        _nsa_compress_kernel,
        grid=(2, g, b),
        in_specs=[pl.BlockSpec((None, None, None, nhalf, zw), lambda kv, gi, bi: (kv, gi, bi, 0, 0)),
                  sq(zw, 2 * HEAD_DIM), sq(8, 2 * zw), sq(2 * zw, HEAD_DIM), sq(HEAD_DIM, HEAD_DIM),
                  pl.BlockSpec((4, HEAD_DIM), lambda kv, gi, bi: (0, 0))],
        out_specs=pl.BlockSpec((None, None, None, nhalf, HEAD_DIM),
                               lambda kv, gi, bi: (kv, gi, bi, 0, 0)),
        out_shape=jax.ShapeDtypeStruct((2, g, b, nhalf, HEAD_DIM), F32),
        compiler_params=_cparams("parallel", "parallel", "parallel"),
        name="nsa_compress",
    )(z, w1ab, pe8, w1flat, w2, qk_gain)


def _masked_softmax_rows(s, mask):
    s = jnp.where(mask, s, NEG_INF)
    e = jnp.where(mask, jnp.exp(s - jnp.max(s, axis=-1, keepdims=True)), 0.0)
    l = jnp.sum(e, axis=-1, keepdims=True)
    return e / jnp.where(l > 0.0, l, 1.0)


def _nsa_attn_kernel(q_ref, gl_ref, kc_ref, vc_ref, ks_ref, vs_ref, kw_ref, vw_ref, gn_ref, o_ref,
                     *, seq):
    i = pl.program_id(2)
    tq = q_ref.shape[0]
    hg = NSA_GROUP
    t0 = i * tq
    scale = HEAD_DIM ** -0.5
    qg = gn_ref[0:1]
    qb = q_ref[...]
    q = jnp.concatenate(
        [_rms(qb[:, h * HEAD_DIM:(h + 1) * HEAD_DIM], qg) * scale for h in range(hg)],
        axis=0).astype(BF16)
    t_tok = t0 + lax.broadcasted_iota(jnp.int32, (tq, 1), 0)
    t_rows = jnp.concatenate([t_tok] * hg, axis=0)

    ncp = kc_ref.shape[0]
    s = _dot_nt(q, kc_ref[...].astype(BF16))
    n_idx = lax.broadcasted_iota(jnp.int32, (1, ncp), 1)
    cmp_valid = (n_idx * CMP_STRIDE + (CMP_BLOCK - 1)) <= t_rows
    p_cmp = _masked_softmax_rows(s, cmp_valid)
    o_cmp = _dot(p_cmp.astype(BF16), vc_ref[...].astype(BF16))

    nsp = LANES
    p_sum = p_cmp[0:tq]
    for h in range(1, hg):
        p_sum = p_sum + p_cmp[h * tq:(h + 1) * tq]
    ov_n = lax.broadcasted_iota(jnp.int32, (ncp, nsp), 0) * CMP_STRIDE
    ov_s = lax.broadcasted_iota(jnp.int32, (ncp, nsp), 1) * SLC_BLOCK
    overlap = ((ov_n < ov_s + SLC_BLOCK) & (ov_n + CMP_BLOCK > ov_s)).astype(F32)
    imp = _dot(p_sum, overlap, HIGHEST)
    s_idx = lax.broadcasted_iota(jnp.int32, (1, nsp), 1)
    cur = jnp.right_shift(t_tok, SLC_SHIFT)
    forced = (s_idx == 0) | (s_idx == cur) | (s_idx == cur - 1)
    imp = jnp.where(forced, FORCE_SCORE, imp)
    imp = jnp.where(s_idx > cur, -1.0, imp)
    work = imp.T
    blk = lax.broadcasted_iota(jnp.int32, (nsp, tq), 0).astype(F32)
    sel_t = jnp.zeros((nsp, tq), F32)
    for _ in range(SLC_TOPK):
        mx = jnp.max(work, axis=0, keepdims=True)
        first = jnp.min(jnp.where(work == mx, blk, float(nsp)), axis=0, keepdims=True)
        chosen = blk == first
        sel_t = jnp.where(chosen, 1.0, sel_t)
        work = jnp.where(chosen, -3e38, work)
    sel = sel_t.T.astype(BF16)

    tk = SLC_KV_TILE
    n_kv = (t0 + tq + tk - 1) // tk

    def slc_body(j, carry):
        m_i, l_i, acc = carry
        k0 = pl.multiple_of(j * tk, tk)
        k = ks_ref[pl.ds(k0, tk), :]
        v = vs_ref[pl.ds(k0, tk), :]
        sc = _dot_nt(q, k)
        kpos = k0 + lax.broadcasted_iota(jnp.int32, (1, tk), 1)
        e_blk = lax.broadcasted_iota(jnp.int32, (nsp, tk), 0)
        e_key = jnp.right_shift(k0 + lax.broadcasted_iota(jnp.int32, (nsp, tk), 1), SLC_SHIFT)
        expand = (e_blk == e_key).astype(BF16)
        picked = _dot(sel, expand) > 0.5
        bias1 = jnp.where(picked & (kpos <= t_tok), 0.0, NEG_INF)
        sc = sc + jnp.concatenate([bias1] * hg, axis=0)
        m_new = jnp.maximum(m_i, jnp.max(sc, axis=-1, keepdims=True))
        alpha = jnp.exp(m_i - m_new)
        p = jnp.exp(sc - jnp.maximum(m_new, 0.1 * NEG_INF))
        l_new = alpha * l_i + jnp.sum(p, axis=-1, keepdims=True)
        acc_new = alpha * acc + _dot(p.astype(BF16), v)
        return m_new, l_new, acc_new

    init = (jnp.full((hg * tq, 1), NEG_INF, F32), jnp.zeros((hg * tq, 1), F32),
            jnp.zeros((hg * tq, HEAD_DIM), F32))
    _, l_s, acc_s = lax.fori_loop(0, n_kv, slc_body, init)
    o_slc = acc_s / l_s

    span = WINDOW + tq
    w0 = pl.multiple_of(jnp.maximum(t0 - WINDOW, 0), tq)
    kw = kw_ref[pl.ds(w0, span), :]
    vw = vw_ref[pl.ds(w0, span), :]
    sw = _dot_nt(q, kw)
    dist = t_rows - (w0 + lax.broadcasted_iota(jnp.int32, (1, span), 1))
    p_win = _masked_softmax_rows(sw, (dist >= 0) & (dist < WINDOW))
    o_win = _dot(p_win.astype(BF16), vw)

    gates = _sigmoid(gl_ref[...])
    outs = []
    for h in range(hg):
        r = slice(h * tq, (h + 1) * tq)
        outs.append(gates[:, 3 * h:3 * h + 1] * o_cmp[r] + gates[:, 3 * h + 1:3 * h + 2] * o_slc[r]
                    + gates[:, 3 * h + 2:3 * h + 3] * o_win[r])
    o_ref[...] = jnp.concatenate(outs, axis=1)


def _nsa_attn(u_nsa, cmp_kv, ksn, vs, kwn, vw, qk_gain, batch, seq):
    m = u_nsa.shape[0]
    g = NSA_KV_HEADS
    nq = seq // Q_TILE
    ncp = cmp_kv.shape[3]
    gw = NSA_GROUP * HEAD_DIM
    gate_col0 = (NSA_W + 6 * KV_W) // LANES
    cmp_spec = lambda kv: pl.BlockSpec((None, None, None, ncp, HEAD_DIM),
                                       lambda b, gi, i: (kv, gi, b, 0, 0))
    kv_spec = pl.BlockSpec((None, seq, HEAD_DIM), lambda b, gi, i: (gi, b, 0))
    return pl.pallas_call(
        functools.partial(_nsa_attn_kernel, seq=seq),
        grid=(batch, g, nq),
        in_specs=[pl.BlockSpec((Q_TILE, gw), lambda b, gi, i: (b * nq + i, gi)),
                  pl.BlockSpec((Q_TILE, LANES), lambda b, gi, i: (b * nq + i, gate_col0 + gi)),
                  cmp_spec(0), cmp_spec(1), kv_spec, kv_spec, kv_spec, kv_spec,
                  pl.BlockSpec((4, HEAD_DIM), lambda b, gi, i: (0, 0))],
        out_specs=pl.BlockSpec((Q_TILE, gw), lambda b, gi, i: (b * nq + i, gi)),
        out_shape=jax.ShapeDtypeStruct((m, NSA_W), F32),
        compiler_params=_cparams("parallel", "parallel", "arbitrary"),
        name="nsa_attn",
    )(u_nsa, u_nsa, cmp_kv, cmp_kv, ksn, vs, kwn, vw, qk_gain)


def _rwkv_prep_kernel(*refs, tiles_per_seq, has_vres):
    if has_vres:
        (z_ref, zp_ref, mu_ref, vec_ref, wup_ref, aup_ref, gup_ref, seg_ref, vf_ref, v0_ref, v1_ref,
         v2_ref, r_o, lw_o, k_o, v_o, kk_o, a_o, g_o, bonus_o) = refs
    else:
        (z_ref, zp_ref, mu_ref, vec_ref, wup_ref, aup_ref, gup_ref, seg_ref,
         r_o, lw_o, k_o, v_o, kk_o, a_o, g_o, bonus_o) = refs
    i = pl.program_id(0)
    z = z_ref[...]
    tm = z.shape[0]
    prev_last = jnp.where(i % tiles_per_seq == 0, 0.0, zp_ref[7:8, :])
    row = lax.broadcasted_iota(jnp.int32, (tm, 1), 0)
    z_prev = jnp.where(row == 0, prev_last, pltpu.roll(z, 1, axis=0))
    z = z + mu_ref[...] * (z_prev - z)
    w = RW_W
    r = z[:, 0:w]
    k = z[:, w:2 * w]
    v = z[:, 2 * w:3 * w]
    wd = z[:, 3 * w:3 * w + DECAY_RANK]
    ad = z[:, 3 * w + DECAY_RANK:3 * w + DECAY_RANK + ICLR_RANK]
    gd = z[:, 3 * w + DECAY_RANK + ICLR_RANK:]
    vec = vec_ref[...]
    w0, a0, k_k, k_a, r_k = (vec[n:n + 1] for n in range(5))
    if has_vres:
        lo = _dot(_dot(v.astype(BF16), v1_ref[...]).astype(BF16), v2_ref[...])
        v = v + (vf_ref[...] - v) * _sigmoid(v0_ref[...] + lo)
    wl = w0 + _dot(jnp.tanh(wd).astype(BF16), wup_ref[...])
    neg = -wl
    softplus = jnp.maximum(neg, 0.0) + jnp.log(1.0 + jnp.exp(-jnp.abs(neg)))
    lw_o[...] = -jnp.exp(-softplus - 0.5)
    a = _sigmoid(a0 + _dot(ad.astype(BF16), aup_ref[...]))
    g_o[...] = _dot(_sigmoid(gd).astype(BF16), gup_ref[...])
    seg = seg_ref[...]
    kk = k * k_k
    nrm = jnp.sqrt(_dot(kk * kk, seg, HIGHEST))
    kk_o[...] = kk / jnp.maximum(nrm, 1e-12)
    kh = k * (1.0 + (a - 1.0) * k_a)
    bonus_o[...] = _dot(r * kh * r_k, seg, HIGHEST) * v
    r_o[...] = r
    k_o[...] = kh
    v_o[...] = v
    a_o[...] = a


def _rwkv_prep(u_rw, mu, vec, wup, aup, gup, seg, vres, seq, tm=256):
    m = u_rw.shape[0]
    full = lambda i: (0, 0)
    row = lambda i: (i, 0)
    in_specs = [pl.BlockSpec((tm, RW_COLS), row),
                pl.BlockSpec((8, RW_COLS), lambda i: (jnp.maximum(i * (tm // 8) - 1, 0), 0)),
                pl.BlockSpec((1, RW_COLS), full), pl.BlockSpec((8, RW_W), full),
                pl.BlockSpec((DECAY_RANK, RW_W), full), pl.BlockSpec((ICLR_RANK, RW_W), full),
                pl.BlockSpec((GATE_RANK, RW_W), full), pl.BlockSpec((RW_W, RW_W), full)]
    args = [u_rw, u_rw, mu, vec, wup, aup, gup, seg]
    if vres is not None:
        v_first, v0, v1, v2 = vres
        in_specs += [pl.BlockSpec((tm, RW_W), row), pl.BlockSpec((1, RW_W), full),
                     pl.BlockSpec((RW_W, VRES_RANK), full), pl.BlockSpec((VRES_RANK, RW_W), full)]
        args += [v_first, v0, v1, v2]
    return pl.pallas_call(
        functools.partial(_rwkv_prep_kernel, tiles_per_seq=seq // tm, has_vres=vres is not None),
        grid=(m // tm,),
        in_specs=in_specs,
        out_specs=[pl.BlockSpec((tm, RW_W), row)] * 8,
        out_shape=[jax.ShapeDtypeStruct((m, RW_W), F32)] * 8,
        compiler_params=_cparams("parallel"),
        name="rwkv_prep",
    )(*args)


def _unit_lower_solve(l_strict, rhs, prec):
    c = l_strict.shape[0]
    ri = lax.broadcasted_iota(jnp.int32, (c, c), 0)
    ci = lax.broadcasted_iota(jnp.int32, (c, c), 1)
    eye = (ri == ci).astype(F32)
    same_blk = (ri // RW_SUB) == (ci // RW_SUB)
    ld = jnp.where(same_blk, l_strict, 0.0)
    lo = l_strict - ld
    d = eye + ld
    p = ld
    sq = 1
    while 2 * sq < RW_SUB:
        p = _dot(p, p, prec)
        d = d + _dot(d, p, prec)
        sq *= 2
    x = _dot(d, rhs, prec)
    n = _dot(d, lo, prec)
    nblk = c // RW_SUB
    terms = 1
    while terms < nblk:
        x = x + _dot(n, x, prec)
        terms *= 2
        if terms < nblk:
            n = _dot(n, n, prec)
    return x


def _rwkv_chunk_kernel(r_ref, lw_ref, k_ref, v_ref, kk_ref, a_ref, g_ref, bonus_ref, lnw_ref,
                       lnb_ref, o_ref, state_ref, *, prec):
    t = pl.program_id(2)

    @pl.when(t == 0)
    def _():
        state_ref[...] = jnp.zeros_like(state_ref)

    rt_rows = r_ref.shape[0]
    c = RW_CHUNK
    ri = lax.broadcasted_iota(jnp.int32, (c, c), 0)
    ci = lax.broadcasted_iota(jnp.int32, (c, c), 1)
    tril_incl = ri >= ci
    tril_strict = ri > ci
    ones_incl = tril_incl.astype(F32)

    for hh in range(LANES // HEAD_DIM):
        ls = slice(hh * HEAD_DIM, (hh + 1) * HEAD_DIM)
        lnw = lnw_ref[:, ls]
        lnb = lnb_ref[:, ls]
        state = state_ref[hh]
        for cb in range(rt_rows // c):
            rs = slice(cb * c, (cb + 1) * c)
            r = r_ref[rs, ls]
            lw = lw_ref[rs, ls]
            k = k_ref[rs, ls]
            v = v_ref[rs, ls]
            kk = kk_ref[rs, ls]
            a = a_ref[rs, ls]
            gcum = _dot(ones_incl, lw, HIGHEST)
            g_last = gcum[c - 1:c]
            e_neg = jnp.exp(-gcum)
            a_t = (-kk) * jnp.exp(gcum - lw)
            b_t = (kk * a) * e_neg
            k_t = k * e_neg
            r_t = r * jnp.exp(gcum)
            b_end = (kk * a) * jnp.exp(g_last - gcum)
            k_end = k * jnp.exp(g_last - gcum)
            l_ab = jnp.where(tril_strict, _dot_nt(a_t, b_t, prec), 0.0)
            l_ak = jnp.where(tril_strict, _dot_nt(a_t, k_t, prec), 0.0)
            m_rb = jnp.where(tril_incl, _dot_nt(r_t, b_t, prec), 0.0)
            m_rk = jnp.where(tril_incl, _dot_nt(r_t, k_t, prec), 0.0)
            rhs = _dot_nt(a_t, state, prec) + _dot(l_ak, v, prec)
            u = _unit_lower_solve(l_ab, rhs, prec)
            y = _dot_nt(r_t, state, prec) + _dot(m_rb, u, prec) + _dot(m_rk, v, prec)
            state = (state * jnp.exp(g_last) + _dot_tn(u, b_end, prec)
                     + _dot_tn(v, k_end, prec))
            mean = jnp.mean(y, axis=-1, keepdims=True)
            yc = y - mean
            var = jnp.mean(yc * yc, axis=-1, keepdims=True)
            yn = yc * lax.rsqrt(var + LNX_EPS) * lnw + lnb
            o_ref[rs, ls] = (yn + bonus_ref[rs, ls]) * g_ref[rs, ls]
        state_ref[hh] = state


def _rwkv_chunk(r, lw, k, v, kk, a, g, bonus, lnw, lnb, batch, seq, prec, rt=512):
    m = r.shape[0]
    nt = seq // rt
    pairs = RW_W // LANES
    row = pl.BlockSpec((rt, LANES), lambda b, p, t: (b * nt + t, p))
    vec = pl.BlockSpec((1, LANES), lambda b, p, t: (0, p))
    return pl.pallas_call(
        functools.partial(_rwkv_chunk_kernel, prec=prec),
        grid=(batch, pairs, nt),
        in_specs=[row] * 8 + [vec, vec],
        out_specs=row,
        out_shape=jax.ShapeDtypeStruct((m, RW_W), F32),
        scratch_shapes=[pltpu.VMEM((LANES // HEAD_DIM, HEAD_DIM, HEAD_DIM), F32)],
        compiler_params=_cparams("parallel", "parallel", "arbitrary"),
        name="rwkv_chunk",
    )(r, lw, k, v, kk, a, g, bonus, lnw, lnb)


def _merge_kernel(x_ref, on_ref, or_ref, mg_ref, pa_ref, pb_ref, wo_ref, o_ref):
    d = x_ref.shape[1]
    ya = _dot(on_ref[...].astype(BF16), pa_ref[...])
    yb = _dot(or_ref[...].astype(BF16), pb_ref[...])
    y = _sigmoid(mg_ref[:, :d]) * ya + _sigmoid(mg_ref[:, d:]) * yb
    o_ref[...] = x_ref[...] + _dot(y.astype(BF16), wo_ref[...])


def _merge(x, o_nsa, o_rw, u_mg, pa, pb, wo, tm=512):
    m, d = x.shape
    full = lambda i: (0, 0)
    row = lambda i: (i, 0)
    return pl.pallas_call(
        _merge_kernel,
        grid=(m // tm,),
        in_specs=[pl.BlockSpec((tm, d), row), pl.BlockSpec((tm, NSA_W), row),
                  pl.BlockSpec((tm, RW_W), row), pl.BlockSpec((tm, 2 * d), row),
                  pl.BlockSpec((NSA_W, d), full), pl.BlockSpec((RW_W, d), full),
                  pl.BlockSpec((d, d), full)],
        out_specs=pl.BlockSpec((tm, d), row),
        out_shape=jax.ShapeDtypeStruct((m, d), F32),
        compiler_params=_cparams("parallel"),
        name="merge_out",
    )(x, o_nsa, o_rw, u_mg, pa, pb, wo)


FFN_HALO = 16


def _ffn_kernel(x_ref, xp_ref, g_ref, wa_ref, wb_ref, ca_ref, cb_ref, wd_ref, o_ref, h_scr, *,
                tiles_per_seq):
    i = pl.program_id(0)
    j = pl.program_id(1)

    @pl.when(j == 0)
    def _():
        x = x_ref[...]
        h_scr[FFN_HALO:, :] = _rms(x, g_ref[...]).astype(BF16)
        hp = _rms(xp_ref[...], g_ref[...])
        h_scr[0:FFN_HALO, :] = jnp.where(i % tiles_per_seq == 0, 0.0, hp).astype(BF16)
        o_ref[...] = x

    h = h_scr[...]

    def conv(u, c_ref):
        cw = c_ref[...]
        return (cw[2:3] * u[FFN_HALO:] + cw[1:2] * pltpu.roll(u, 1, axis=0)[FFN_HALO:]
                + cw[0:1] * pltpu.roll(u, 2, axis=0)[FFN_HALO:])

    a = conv(_dot(h, wa_ref[...]), ca_ref)
    b = conv(_dot(h, wb_ref[...]), cb_ref)
    o_ref[...] += _dot((_silu(a) * b).astype(BF16), wd_ref[...])


def _ffn(x, gain, w_up, conv_w, w_down, seq, tm=512, tf=1408):
    m, d = x.shape
    nf = D_FF // tf
    return pl.pallas_call(
        functools.partial(_ffn_kernel, tiles_per_seq=seq // tm),
        grid=(m // tm, nf),
        in_specs=[pl.BlockSpec((tm, d), lambda i, j: (i, 0)),
                  pl.BlockSpec((FFN_HALO, d),
                               lambda i, j: (jnp.maximum(i * (tm // FFN_HALO) - 1, 0), 0)),
                  pl.BlockSpec((1, d), lambda i, j: (0, 0)),
                  pl.BlockSpec((d, tf), lambda i, j: (0, j)),
                  pl.BlockSpec((d, tf), lambda i, j: (0, j + nf)),
                  pl.BlockSpec((3, tf), lambda i, j: (0, j)),
                  pl.BlockSpec((3, tf), lambda i, j: (0, j + nf)),
                  pl.BlockSpec((tf, d), lambda i, j: (j, 0))],
        out_specs=pl.BlockSpec((tm, d), lambda i, j: (i, 0)),
        out_shape=jax.ShapeDtypeStruct((m, d), F32),
        scratch_shapes=[pltpu.VMEM((tm + FFN_HALO, d), BF16)],
        compiler_params=_cparams("parallel", "arbitrary"),
        name="conv_ffn",
    )(x, x, gain, w_up, w_up, conv_w, conv_w, w_down)


def _split_w_in(w):
    o = 0
    q = w[:, o:o + NSA_W]; o += NSA_W
    kv = w[:, o:o + 6 * KV_W]; o += 6 * KV_W
    n_gate = NSA_HEADS * 3
    gl = w[:, o:o + n_gate]; o += n_gate
    rw = w[:, o:o + RW_COLS]; o += RW_COLS
    mg = w[:, o:]
    per_g = n_gate // NSA_KV_HEADS
    pad = jnp.zeros((w.shape[0], LANES - per_g), w.dtype)
    gates = [jnp.concatenate([gl[:, g * per_g:(g + 1) * per_g], pad], axis=1)
             for g in range(NSA_KV_HEADS)]
    w_nsa = jnp.concatenate([q, kv] + gates, axis=1)
    return w_nsa.astype(BF16), rw.astype(BF16), mg.astype(BF16)


def kernel(x, norm_mix, norm_ffn, w_in, qk_gain, cmp_pe, cmp_w1, cmp_w2, rwkv_mu, rwkv_w0, rwkv_w_up,
           rwkv_a0, rwkv_a_up, rwkv_g_up, rwkv_k_k, rwkv_k_a, rwkv_r_k, rwkv_ln_w, rwkv_ln_b, vres_v0,
           vres_v1, vres_v2, proj_nsa, proj_rwkv, w_out, ffn_up, ffn_conv, ffn_down):
    batch, seq, d = x.shape
    depth = w_in.shape[0]
    m = batch * seq
    assert seq % SLC_KV_TILE == 0 and seq // SLC_BLOCK <= LANES and seq >= WINDOW + Q_TILE
    xf = x.reshape(m, d)
    half = CMP_STRIDE * HEAD_DIM
    hi = lax.broadcasted_iota(jnp.int32, (RW_W, RW_W), 0) // HEAD_DIM
    hj = lax.broadcasted_iota(jnp.int32, (RW_W, RW_W), 1) // HEAD_DIM
    seg = (hi == hj).astype(F32)
    v_first = None
    for l in range(depth):
        w_nsa, w_rw, w_mg = _split_w_in(w_in[l])
        u_nsa, u_rw, u_mg = _inproj(xf, norm_mix[l][None], w_nsa, w_rw, w_mg)

        cmp_in, ksn, vs, kwn, vw = _nsa_prep(u_nsa, qk_gain[l])
        z = cmp_in.reshape(2, NSA_KV_HEADS, batch, seq // CMP_STRIDE, half)
        w1 = cmp_w1[l]
        w1ab = jnp.concatenate([w1[:, :CMP_STRIDE].reshape(2, half, HEAD_DIM),
                                w1[:, CMP_STRIDE:].reshape(2, half, HEAD_DIM)], axis=2).astype(BF16)
        w1flat = w1.reshape(2, 2 * half, HEAD_DIM).astype(BF16)
        pe8 = jnp.broadcast_to(cmp_pe[l].reshape(2, 1, 2 * half), (2, 8, 2 * half))
        cmp_kv = _nsa_compress(z, w1ab, pe8, w1flat, cmp_w2[l].astype(BF16), qk_gain[l])
        o_nsa = _nsa_attn(u_nsa, cmp_kv, ksn, vs, kwn, vw, qk_gain[l], batch, seq)

        zero = jnp.zeros((RW_W,), F32)
        vec = jnp.stack([rwkv_w0[l], rwkv_a0[l], rwkv_k_k[l], rwkv_k_a[l], rwkv_r_k[l].reshape(RW_W),
                         zero, zero, zero])
        vres = None
        if l > 0:
            vres = (v_first, vres_v0[l - 1][None], vres_v1[l - 1].astype(BF16),
                    vres_v2[l - 1].astype(BF16))
        r, lw, kh, v, kk, a, g, bonus = _rwkv_prep(
            u_rw, rwkv_mu[l][None], vec, rwkv_w_up[l].astype(BF16), rwkv_a_up[l].astype(BF16),
            rwkv_g_up[l].astype(BF16), seg, vres, seq)
        if l == 0:
            v_first = v
        o_rw = _rwkv_chunk(r, lw, kh, v, kk, a, g, bonus, rwkv_ln_w[l][None], rwkv_ln_b[l][None],
                           batch, seq, HIGHEST)

        xf = _merge(xf, o_nsa, o_rw, u_mg, proj_nsa[l].astype(BF16), proj_rwkv[l].astype(BF16),
                    w_out[l].astype(BF16))
        xf = _ffn(xf, norm_ffn[l][None], ffn_up[l].astype(BF16), ffn_conv[l],
                  ffn_down[l].astype(BF16), seq)
    return xf.reshape(batch, seq, d)
```

```python
import functools

import jax
import jax.numpy as jnp
from jax import lax
from jax.experimental import pallas as pl
from jax.experimental.pallas import tpu as pltpu

F32 = jnp.float32
BF16 = jnp.bfloat16
HIGHEST = lax.Precision.HIGHEST

LANES = 128
VMEM_LIMIT_BYTES = 56 * 1024 * 1024

NSA_HEADS = 8
NSA_KV_HEADS = 2
NSA_GROUP = NSA_HEADS // NSA_KV_HEADS
HEAD_DIM = 64
CMP_BLOCK = 32
CMP_STRIDE = 16
SLC_BLOCK = 64
SLC_SHIFT = 6
SLC_TOPK = 16
WINDOW = 512
FORCE_SCORE = 1e4
NEG_INF = -1e30
RWKV_HEADS = 8
DECAY_RANK = 64
ICLR_RANK = 64
GATE_RANK = 128
VRES_RANK = 32
LNX_EPS = 1e-5 * HEAD_DIM
D_FF = 2816
EPS = 1e-6

NSA_W = NSA_HEADS * HEAD_DIM
KV_W = NSA_KV_HEADS * HEAD_DIM
RW_W = RWKV_HEADS * HEAD_DIM
RW_COLS = 3 * RW_W + DECAY_RANK + ICLR_RANK + GATE_RANK
NSA_SLAB = NSA_W + 6 * KV_W + NSA_KV_HEADS * LANES

Q_TILE = 128
SLC_KV_TILE = 512
RW_CHUNK = 64
RW_SUB = 16


def _cparams(*sem):
    return pltpu.CompilerParams(dimension_semantics=sem, vmem_limit_bytes=VMEM_LIMIT_BYTES)


def _dot(a, b, precision=None):
    return lax.dot_general(a, b, (((1,), (0,)), ((), ())), precision=precision,
                           preferred_element_type=F32)


def _dot_nt(a, b, precision=None):
    return lax.dot_general(a, b, (((1,), (1,)), ((), ())), precision=precision,
                           preferred_element_type=F32)


def _dot_tn(a, b, precision=None):
    return lax.dot_general(a, b, (((0,), (0,)), ((), ())), precision=precision,
                           preferred_element_type=F32)


def _rms(x, gain):
    return x * lax.rsqrt(jnp.mean(x * x, axis=-1, keepdims=True) + EPS) * gain


def _sigmoid(x):
    return 1.0 / (1.0 + jnp.exp(-x))


def _silu(x):
    return x * _sigmoid(x)


def _inproj_kernel(x_ref, g_ref, wn_ref, wr_ref, wm_ref, on_ref, or_ref, om_ref):
    h = _rms(x_ref[...], g_ref[...]).astype(BF16)
    on_ref[...] = _dot(h, wn_ref[...])
    or_ref[...] = _dot(h, wr_ref[...])
    om_ref[...] = _dot(h, wm_ref[...])


def _inproj(x, gain, w_nsa, w_rw, w_mg, tm=256):
    m, d = x.shape
    outs = [w_nsa.shape[1], w_rw.shape[1], w_mg.shape[1]]
    full = lambda i: (0, 0)
    return pl.pallas_call(
        _inproj_kernel,
        grid=(m // tm,),
        in_specs=[pl.BlockSpec((tm, d), lambda i: (i, 0)), pl.BlockSpec((1, d), full)]
        + [pl.BlockSpec((d, n), full) for n in outs],
        out_specs=[pl.BlockSpec((tm, n), lambda i: (i, 0)) for n in outs],
        out_shape=[jax.ShapeDtypeStruct((m, n), F32) for n in outs],
        compiler_params=_cparams("parallel"),
        name="inproj",
    )(x, gain, w_nsa, w_rw, w_mg)


def _nsa_prep_kernel(kc_ref, vc_ref, ks_ref, vs_ref, kw_ref, vw_ref, gn_ref, cmp_ref, ksn_ref,
                     vs_o_ref, kwn_ref, vw_o_ref):
    gn = gn_ref[...]
    for g in range(NSA_KV_HEADS):
        sl = slice(g * HEAD_DIM, (g + 1) * HEAD_DIM)
        cmp_ref[0, g] = kc_ref[:, sl]
        cmp_ref[1, g] = vc_ref[:, sl]
        ksn_ref[g] = _rms(ks_ref[:, sl], gn[2:3]).astype(BF16)
        kwn_ref[g] = _rms(kw_ref[:, sl], gn[3:4]).astype(BF16)
        vs_o_ref[g] = vs_ref[:, sl].astype(BF16)
        vw_o_ref[g] = vw_ref[:, sl].astype(BF16)


def _nsa_prep(u_nsa, qk_gain, tm=512):
    m = u_nsa.shape[0]
    g = NSA_KV_HEADS
    col0 = NSA_W // KV_W
    in_specs = [pl.BlockSpec((tm, KV_W), functools.partial(lambda i, c: (i, c), c=col0 + c))
                for c in range(6)]
    in_specs.append(pl.BlockSpec((4, HEAD_DIM), lambda i: (0, 0)))
    kv_spec = pl.BlockSpec((g, tm, HEAD_DIM), lambda i: (0, i, 0))
    kv_shape = jax.ShapeDtypeStruct((g, m, HEAD_DIM), BF16)
    return pl.pallas_call(
        _nsa_prep_kernel,
        grid=(m // tm,),
        in_specs=in_specs,
        out_specs=[pl.BlockSpec((2, g, tm, HEAD_DIM), lambda i: (0, 0, i, 0)), kv_spec, kv_spec,
                   kv_spec, kv_spec],
        out_shape=[jax.ShapeDtypeStruct((2, g, m, HEAD_DIM), F32), kv_shape, kv_shape, kv_shape,
                   kv_shape],
        compiler_params=_cparams("parallel"),
        name="nsa_prep",
    )(*([u_nsa] * 6), qk_gain)


def _nsa_compress_kernel(z_ref, w1ab_ref, pe_ref, w1_ref, w2_ref, gn_ref, o_ref):
    kv = pl.program_id(0)
    nhalf = z_ref.shape[0]
    p = _dot(z_ref[...].astype(BF16), w1ab_ref[...])
    a = p[:, :HEAD_DIM]
    b_next = pltpu.roll(p[:, HEAD_DIM:], nhalf - 1, axis=0)
    c = _dot(pe_ref[...].astype(BF16), w1_ref[...])[0:1]
    hid = _silu(a + b_next + c)
    o = _dot(hid.astype(BF16), w2_ref[...])
    o_ref[...] = jnp.where(kv == 0, _rms(o, gn_ref[1:2]), o)


def _nsa_compress(z, w1ab, pe8, w1flat, w2, qk_gain):
    _, g, b, nhalf, zw = z.shape
    sq = lambda *shape: pl.BlockSpec((None,) + shape, lambda kv, gi, bi: (kv,) + (0,) * len(shape))
    return pl.pallas_call(
        _nsa_compress_kernel,
        grid=(2, g, b),
        in_specs=[pl.BlockSpec((None, None, None, nhalf, zw), lambda kv, gi, bi: (kv, gi, bi, 0, 0)),
                  sq(zw, 2 * HEAD_DIM), sq(8, 2 * zw), sq(2 * zw, HEAD_DIM), sq(HEAD_DIM, HEAD_DIM),
                  pl.BlockSpec((4, HEAD_DIM), lambda kv, gi, bi: (0, 0))],
        out_specs=pl.BlockSpec((None, None, None, nhalf, HEAD_DIM),
                               lambda kv, gi, bi: (kv, gi, bi, 0, 0)),
        out_shape=jax.ShapeDtypeStruct((2, g, b, nhalf, HEAD_DIM), F32),
        compiler_params=_cparams("parallel", "parallel", "parallel"),
        name="nsa_compress",
    )(z, w1ab, pe8, w1flat, w2, qk_gain)


def _masked_softmax_rows(s, mask):
    s = jnp.where(mask, s, NEG_INF)
    e = jnp.where(mask, jnp.exp(s - jnp.max(s, axis=-1, keepdims=True)), 0.0)
    l = jnp.sum(e, axis=-1, keepdims=True)
    return e / jnp.where(l > 0.0, l, 1.0)


def _nsa_attn_kernel(q_ref, gl_ref, kc_ref, vc_ref, ks_ref, vs_ref, kw_ref, vw_ref, gn_ref, o_ref,
                     *, seq):
    i = pl.program_id(2)
    tq = q_ref.shape[0]
    hg = NSA_GROUP
    t0 = i * tq
    scale = HEAD_DIM ** -0.5
    qg = gn_ref[0:1]
    qb = q_ref[...]
    q = jnp.concatenate(
        [_rms(qb[:, h * HEAD_DIM:(h + 1) * HEAD_DIM], qg) * scale for h in range(hg)],
        axis=0).astype(BF16)
    t_tok = t0 + lax.broadcasted_iota(jnp.int32, (tq, 1), 0)
    t_rows = jnp.concatenate([t_tok] * hg, axis=0)

    ncp = kc_ref.shape[0]
    s = _dot_nt(q, kc_ref[...].astype(BF16))
    n_idx = lax.broadcasted_iota(jnp.int32, (1, ncp), 1)
    cmp_valid = (n_idx * CMP_STRIDE + (CMP_BLOCK - 1)) <= t_rows
    p_cmp = _masked_softmax_rows(s, cmp_valid)
    o_cmp = _dot(p_cmp.astype(BF16), vc_ref[...].astype(BF16))

    nsp = LANES
    p_sum = p_cmp[0:tq]
    for h in range(1, hg):
        p_sum = p_sum + p_cmp[h * tq:(h + 1) * tq]
    ov_n = lax.broadcasted_iota(jnp.int32, (ncp, nsp), 0) * CMP_STRIDE
    ov_s = lax.broadcasted_iota(jnp.int32, (ncp, nsp), 1) * SLC_BLOCK
    overlap = jnp.where((ov_n < ov_s + SLC_BLOCK) & (ov_n + CMP_BLOCK > ov_s), 1.0, 0.0).astype(BF16)
    p_hi = p_sum.astype(BF16)
    r1 = p_sum - p_hi.astype(F32)
    p_mid = r1.astype(BF16)
    p_lo = (r1 - p_mid.astype(F32)).astype(BF16)
    imp3 = _dot(jnp.concatenate([p_hi, p_mid, p_lo], axis=0), overlap)
    imp = imp3[0:tq] + imp3[tq:2 * tq] + imp3[2 * tq:3 * tq]
    s_idx = lax.broadcasted_iota(jnp.int32, (1, nsp), 1)
    cur = jnp.right_shift(t_tok, SLC_SHIFT)
    forced = (s_idx == 0) | (s_idx == cur) | (s_idx == cur - 1)
    imp = jnp.where(forced, FORCE_SCORE, imp)
    imp = jnp.where(s_idx > cur, -1.0, imp)

    span = WINDOW + tq
    w0 = pl.multiple_of(jnp.maximum(t0 - WINDOW, 0), tq)
    kw = kw_ref[pl.ds(w0, span), :]
    vw = vw_ref[pl.ds(w0, span), :]
    sw = _dot_nt(q, kw)
    dist = t_rows - (w0 + lax.broadcasted_iota(jnp.int32, (1, span), 1))
    p_win = _masked_softmax_rows(sw, (dist >= 0) & (dist < WINDOW))
    o_win = _dot(p_win.astype(BF16), vw)

    work = imp.T
    blk = lax.broadcasted_iota(jnp.int32, (nsp, tq), 0).astype(F32)
    sel_t = jnp.zeros((nsp, tq), F32)
    for _ in range(SLC_TOPK):
        mx = jnp.max(work, axis=0, keepdims=True)
        first = jnp.min(jnp.where(work == mx, blk, float(nsp)), axis=0, keepdims=True)
        chosen = blk == first
        sel_t = jnp.where(chosen, 1.0, sel_t)
        work = jnp.where(chosen, -3e38, work)
    sel = sel_t.T.astype(BF16)

    tk = SLC_KV_TILE
    n_kv = (t0 + tq + tk - 1) // tk
    rel_blk = (lax.broadcasted_iota(jnp.int32, (nsp, tk), 0)
               - jnp.right_shift(lax.broadcasted_iota(jnp.int32, (nsp, tk), 1), SLC_SHIFT))
    key_iota = lax.broadcasted_iota(jnp.int32, (1, tk), 1)

    def slc_body(j, carry):
        k0 = pl.multiple_of(j * tk, tk)
        k = ks_ref[pl.ds(k0, tk), :]
        v = vs_ref[pl.ds(k0, tk), :]
        sc = _dot_nt(q, k)
        expand = jnp.where(rel_blk == j * (tk // SLC_BLOCK), 1.0, 0.0).astype(BF16)
        picked = _dot(sel, expand) > 0.5
        bias = jnp.where(picked & (k0 + key_iota <= t_tok), 0.0, NEG_INF)
        m_i, l_i, acc = carry
        sc = sc + jnp.concatenate([bias] * hg, axis=0)
        m_new = jnp.maximum(m_i, jnp.max(sc, axis=-1, keepdims=True))
        alpha = jnp.exp(m_i - m_new)
        p = jnp.exp(sc - jnp.maximum(m_new, 0.1 * NEG_INF))
        l_new = alpha * l_i + jnp.sum(p, axis=-1, keepdims=True)
        return m_new, l_new, alpha * acc + _dot(p.astype(BF16), v)

    init = (jnp.full((hg * tq, 1), NEG_INF, F32), jnp.zeros((hg * tq, 1), F32),
            jnp.zeros((hg * tq, HEAD_DIM), F32))
    _, l_s, acc_s = lax.fori_loop(0, n_kv, slc_body, init)
    o_slc = acc_s / l_s

    gates = _sigmoid(gl_ref[...])
    outs = []
    for h in range(hg):
        r = slice(h * tq, (h + 1) * tq)
        outs.append(gates[:, 3 * h:3 * h + 1] * o_cmp[r] + gates[:, 3 * h + 1:3 * h + 2] * o_slc[r]
                    + gates[:, 3 * h + 2:3 * h + 3] * o_win[r])
    o_ref[...] = jnp.concatenate(outs, axis=1)


def _nsa_attn(u_nsa, cmp_kv, ksn, vs, kwn, vw, qk_gain, batch, seq):
    m = u_nsa.shape[0]
    g = NSA_KV_HEADS
    nq = seq // Q_TILE
    ncp = cmp_kv.shape[3]
    gw = NSA_GROUP * HEAD_DIM
    gate_col0 = (NSA_W + 6 * KV_W) // LANES
    cmp_spec = lambda kv: pl.BlockSpec((None, None, None, ncp, HEAD_DIM),
                                       lambda b, gi, i: (kv, gi, b, 0, 0))
    kv_spec = pl.BlockSpec((None, seq, HEAD_DIM), lambda b, gi, i: (gi, b, 0))
    return pl.pallas_call(
        functools.partial(_nsa_attn_kernel, seq=seq),
        grid=(batch, g, nq),
        in_specs=[pl.BlockSpec((Q_TILE, gw), lambda b, gi, i: (b * nq + i, gi)),
                  pl.BlockSpec((Q_TILE, LANES), lambda b, gi, i: (b * nq + i, gate_col0 + gi)),
                  cmp_spec(0), cmp_spec(1), kv_spec, kv_spec, kv_spec, kv_spec,
                  pl.BlockSpec((4, HEAD_DIM), lambda b, gi, i: (0, 0))],
        out_specs=pl.BlockSpec((Q_TILE, gw), lambda b, gi, i: (b * nq + i, gi)),
        out_shape=jax.ShapeDtypeStruct((m, NSA_W), F32),
        compiler_params=_cparams("parallel", "parallel", "arbitrary"),
        name="nsa_attn",
    )(u_nsa, u_nsa, cmp_kv, cmp_kv, ksn, vs, kwn, vw, qk_gain)


def _rwkv_prep_kernel(*refs, tiles_per_seq, has_vres):
    if has_vres:
        (z_ref, zp_ref, mu_ref, vec_ref, wup_ref, aup_ref, gup_ref, seg_ref, vf_ref, v0_ref, v1_ref,
         v2_ref, r_o, lw_o, k_o, v_o, kk_o, a_o, g_o, bonus_o) = refs
    else:
        (z_ref, zp_ref, mu_ref, vec_ref, wup_ref, aup_ref, gup_ref, seg_ref,
         r_o, lw_o, k_o, v_o, kk_o, a_o, g_o, bonus_o) = refs
    i = pl.program_id(0)
    z = z_ref[...]
    tm = z.shape[0]
    prev_last = jnp.where(i % tiles_per_seq == 0, 0.0, zp_ref[7:8, :])
    row = lax.broadcasted_iota(jnp.int32, (tm, 1), 0)
    z_prev = jnp.where(row == 0, prev_last, pltpu.roll(z, 1, axis=0))
    z = z + mu_ref[...] * (z_prev - z)
    w = RW_W
    r = z[:, 0:w]
    k = z[:, w:2 * w]
    v = z[:, 2 * w:3 * w]
    wd = z[:, 3 * w:3 * w + DECAY_RANK]
    ad = z[:, 3 * w + DECAY_RANK:3 * w + DECAY_RANK + ICLR_RANK]
    gd = z[:, 3 * w + DECAY_RANK + ICLR_RANK:]
    vec = vec_ref[...]
    w0, a0, k_k, k_a, r_k = (vec[n:n + 1] for n in range(5))
    if has_vres:
        lo = _dot(_dot(v.astype(BF16), v1_ref[...]).astype(BF16), v2_ref[...])
        v = v + (vf_ref[...] - v) * _sigmoid(v0_ref[...] + lo)
    wl = w0 + _dot(jnp.tanh(wd).astype(BF16), wup_ref[...])
    neg = -wl
    softplus = jnp.maximum(neg, 0.0) + jnp.log(1.0 + jnp.exp(-jnp.abs(neg)))
    lw_o[...] = -jnp.exp(-softplus - 0.5)
    a = _sigmoid(a0 + _dot(ad.astype(BF16), aup_ref[...]))
    g_o[...] = _dot(_sigmoid(gd).astype(BF16), gup_ref[...])
    seg = seg_ref[...]
    kk = k * k_k
    nrm = jnp.sqrt(_dot(kk * kk, seg, HIGHEST))
    kk_o[...] = kk / jnp.maximum(nrm, 1e-12)
    kh = k * (1.0 + (a - 1.0) * k_a)
    bonus_o[...] = _dot(r * kh * r_k, seg, HIGHEST) * v
    r_o[...] = r
    k_o[...] = kh
    v_o[...] = v
    a_o[...] = a


def _rwkv_prep(u_rw, mu, vec, wup, aup, gup, seg, vres, seq, tm=256):
    m = u_rw.shape[0]
    full = lambda i: (0, 0)
    row = lambda i: (i, 0)
    in_specs = [pl.BlockSpec((tm, RW_COLS), row),
                pl.BlockSpec((8, RW_COLS), lambda i: (jnp.maximum(i * (tm // 8) - 1, 0), 0)),
                pl.BlockSpec((1, RW_COLS), full), pl.BlockSpec((8, RW_W), full),
                pl.BlockSpec((DECAY_RANK, RW_W), full), pl.BlockSpec((ICLR_RANK, RW_W), full),
                pl.BlockSpec((GATE_RANK, RW_W), full), pl.BlockSpec((RW_W, RW_W), full)]
    args = [u_rw, u_rw, mu, vec, wup, aup, gup, seg]
    if vres is not None:
        v_first, v0, v1, v2 = vres
        in_specs += [pl.BlockSpec((tm, RW_W), row), pl.BlockSpec((1, RW_W), full),
                     pl.BlockSpec((RW_W, VRES_RANK), full), pl.BlockSpec((VRES_RANK, RW_W), full)]
        args += [v_first, v0, v1, v2]
    return pl.pallas_call(
        functools.partial(_rwkv_prep_kernel, tiles_per_seq=seq // tm, has_vres=vres is not None),
        grid=(m // tm,),
        in_specs=in_specs,
        out_specs=[pl.BlockSpec((tm, RW_W), row)] * 8,
        out_shape=[jax.ShapeDtypeStruct((m, RW_W), F32)] * 8,
        compiler_params=_cparams("parallel"),
        name="rwkv_prep",
    )(*args)


def _b(x):
    return x.astype(BF16)


def _tri_inverse(l_list, eye):
    c = eye.shape[0]
    ri = lax.broadcasted_iota(jnp.int32, (c, c), 0)
    ci = lax.broadcasted_iota(jnp.int32, (c, c), 1)
    same_blk = (ri // RW_SUB) == (ci // RW_SUB)
    ld = [jnp.where(same_blk, l, 0.0) for l in l_list]
    lo = [_b(l - d0) for l, d0 in zip(l_list, ld)]
    d = [eye + x for x in ld]
    p = [_b(x) for x in ld]
    sq = 1
    while 2 * sq < RW_SUB:
        p = [_b(_dot(x, x)) for x in p]
        d = [x + _dot(_b(x), y) for x, y in zip(d, p)]
        sq *= 2
    d = [_b(x) for x in d]
    n = [_dot(x, y) for x, y in zip(d, lo)]
    x = [eye + y for y in n]
    n = [_b(y) for y in n]
    terms = 2
    while terms < c // RW_SUB:
        n = [_b(_dot(y, y)) for y in n]
        x = [z + _dot(_b(z), y) for z, y in zip(x, n)]
        terms *= 2
    return [_dot(_b(z), y) for z, y in zip(x, d)]


def _rwkv_chunk_kernel(r_ref, lw_ref, k_ref, v_ref, kk_ref, a_ref, g_ref, bonus_ref, lnw_ref,
                       lnb_ref, o_ref, state_ref, *, chain_prec):
    t = pl.program_id(2)

    @pl.when(t == 0)
    def _():
        state_ref[...] = jnp.zeros_like(state_ref)

    rt_rows = r_ref.shape[0]
    c = RW_CHUNK
    hd = HEAD_DIM
    nh = LANES // hd
    ri = lax.broadcasted_iota(jnp.int32, (c, c), 0)
    ci = lax.broadcasted_iota(jnp.int32, (c, c), 1)
    tril_incl = ri >= ci
    tril_strict = ri > ci
    is_diag = ri == ci
    ones_incl = jnp.where(tril_incl, 1.0, 0.0).astype(BF16)
    eye = jnp.where(is_diag, 1.0, 0.0)
    if chain_prec is None:
        chain = lambda x: _b(x)
    else:
        chain = lambda x: x

    nc = rt_rows // c
    rows = [slice(cb * c, (cb + 1) * c) for cb in range(nc)]
    lanes = [slice(hh * hd, (hh + 1) * hd) for hh in range(nh)]
    items = [(cb, hh) for cb in range(nc) for hh in range(nh)]

    def cumulative(lw):
        lw_hi = _b(lw)
        lw_r = lw - lw_hi.astype(F32)
        lw_mid = _b(lw_r)
        lw_lo = _b(lw_r - lw_mid.astype(F32))
        g3 = _dot(ones_incl, jnp.concatenate([lw_hi, lw_mid, lw_lo], axis=1))
        return g3[:, :LANES] + g3[:, LANES:2 * LANES] + g3[:, 2 * LANES:]

    lw = [lw_ref[rs, :] for rs in rows]
    gcum = [cumulative(x) for x in lw]
    g_last = [x[c - 1:c] for x in gcum]
    kk = [kk_ref[rs, :] for rs in rows]
    k = [k_ref[rs, :] for rs in rows]
    v2 = [v_ref[rs, :] for rs in rows]
    b = [kk[cb] * a_ref[rows[cb], :] for cb in range(nc)]
    e_neg = [jnp.exp(-x) for x in gcum]
    e_end = [jnp.exp(g_last[cb] - gcum[cb]) for cb in range(nc)]
    a_t2 = [(-kk[cb]) * jnp.exp(gcum[cb] - lw[cb]) for cb in range(nc)]
    r_t2 = [r_ref[rows[cb], :] * jnp.exp(gcum[cb]) for cb in range(nc)]
    b_t2 = [b[cb] * e_neg[cb] for cb in range(nc)]
    k_t2 = [k[cb] * e_neg[cb] for cb in range(nc)]
    b_end2 = [b[cb] * e_end[cb] for cb in range(nc)]
    k_end2 = [k[cb] * e_end[cb] for cb in range(nc)]
    decay2 = [jnp.exp(x) for x in g_last]

    gram = [_dot_nt(_b(jnp.concatenate([a_t2[cb][:, lanes[hh]], r_t2[cb][:, lanes[hh]]], axis=0)),
                    _b(jnp.concatenate([b_t2[cb][:, lanes[hh]], k_t2[cb][:, lanes[hh]]], axis=0)))
            for cb, hh in items]
    l_ab = [jnp.where(tril_strict, x[:c, :c], 0.0) for x in gram]
    m_rb = [_b(jnp.where(tril_incl, x[c:, :c], 0.0)) for x in gram]
    lm = [_b(jnp.concatenate([jnp.where(tril_strict, x[:c, c:], 0.0),
                              jnp.where(tril_incl, x[c:, c:], 0.0)], axis=0)) for x in gram]
    v_b = [_b(v2[cb][:, lanes[hh]]) for cb, hh in items]
    lmv = [_dot(x, y) for x, y in zip(lm, v_b)]
    tinv = _tri_inverse(l_ab, eye)
    tw = [_dot(_b(tinv[i]), _b(jnp.concatenate([a_t2[cb][:, lanes[hh]], lmv[i][:c]], axis=1)))
          for i, (cb, hh) in enumerate(items)]
    tw_b = [_b(x) for x in tw]
    mb = [_dot(x, y) for x, y in zip(m_rb, tw_b)]
    q_eff = [chain(r_t2[cb][:, lanes[hh]] + mb[i][:, :hd]) for i, (cb, hh) in enumerate(items)]
    y_loc = [mb[i][:, hd:] + lmv[i][c:] for i in range(len(items))]
    a_c = [chain(jnp.where(is_diag, decay2[cb][:, lanes[hh]], 0.0)
                 + _dot_tn(tw_b[i][:, :hd], _b(b_end2[cb][:, lanes[hh]])))
           for i, (cb, hh) in enumerate(items)]
    b_c = [_dot_tn(jnp.concatenate([tw_b[i][:, hd:], v_b[i]], axis=0),
                   _b(jnp.concatenate([b_end2[cb][:, lanes[hh]], k_end2[cb][:, lanes[hh]]], axis=0)))
           for i, (cb, hh) in enumerate(items)]

    states = [state_ref[hh] for hh in range(nh)]
    for i, (cb, hh) in enumerate(items):
        s0 = chain(states[hh])
        y = _dot_nt(q_eff[i], s0, chain_prec) + y_loc[i]
        states[hh] = _dot(s0, a_c[i], chain_prec) + b_c[i]
        rs, ls = rows[cb], lanes[hh]
        mean = jnp.mean(y, axis=-1, keepdims=True)
        yc = y - mean
        var = jnp.mean(yc * yc, axis=-1, keepdims=True)
        yn = yc * lax.rsqrt(var + LNX_EPS) * lnw_ref[:, ls] + lnb_ref[:, ls]
        o_ref[rs, ls] = (yn + bonus_ref[rs, ls]) * g_ref[rs, ls]
    for hh in range(nh):
        state_ref[hh] = states[hh]


def _rwkv_chunk(r, lw, k, v, kk, a, g, bonus, lnw, lnb, batch, seq, chain_prec, rt=512):
    m = r.shape[0]
    nt = seq // rt
    pairs = RW_W // LANES
    row = pl.BlockSpec((rt, LANES), lambda b, p, t: (b * nt + t, p))
    vec = pl.BlockSpec((1, LANES), lambda b, p, t: (0, p))
    return pl.pallas_call(
        functools.partial(_rwkv_chunk_kernel, chain_prec=chain_prec),
        grid=(batch, pairs, nt),
        in_specs=[row] * 8 + [vec, vec],
        out_specs=row,
        out_shape=jax.ShapeDtypeStruct((m, RW_W), F32),
        scratch_shapes=[pltpu.VMEM((LANES // HEAD_DIM, HEAD_DIM, HEAD_DIM), F32)],
        compiler_params=_cparams("parallel", "parallel", "arbitrary"),
        name="rwkv_chunk",
    )(r, lw, k, v, kk, a, g, bonus, lnw, lnb)


def _merge_kernel(x_ref, on_ref, or_ref, mg_ref, pa_ref, pb_ref, wo_ref, o_ref):
    d = x_ref.shape[1]
    ya = _dot(on_ref[...].astype(BF16), pa_ref[...])
    yb = _dot(or_ref[...].astype(BF16), pb_ref[...])
    y = _sigmoid(mg_ref[:, :d]) * ya + _sigmoid(mg_ref[:, d:]) * yb
    o_ref[...] = x_ref[...] + _dot(y.astype(BF16), wo_ref[...])


def _merge(x, o_nsa, o_rw, u_mg, pa, pb, wo, tm=512):
    m, d = x.shape
    full = lambda i: (0, 0)
    row = lambda i: (i, 0)
    return pl.pallas_call(
        _merge_kernel,
        grid=(m // tm,),
        in_specs=[pl.BlockSpec((tm, d), row), pl.BlockSpec((tm, NSA_W), row),
                  pl.BlockSpec((tm, RW_W), row), pl.BlockSpec((tm, 2 * d), row),
                  pl.BlockSpec((NSA_W, d), full), pl.BlockSpec((RW_W, d), full),
                  pl.BlockSpec((d, d), full)],
        out_specs=pl.BlockSpec((tm, d), row),
        out_shape=jax.ShapeDtypeStruct((m, d), F32),
        compiler_params=_cparams("parallel"),
        name="merge_out",
    )(x, o_nsa, o_rw, u_mg, pa, pb, wo)


FFN_HALO = 16


def _ffn_kernel(x_ref, xp_ref, g_ref, wa_ref, wb_ref, ca_ref, cb_ref, wd_ref, o_ref, h_scr, *,
                tiles_per_seq):
    i = pl.program_id(0)
    j = pl.program_id(1)

    @pl.when(j == 0)
    def _():
        x = x_ref[...]
        h_scr[FFN_HALO:, :] = _rms(x, g_ref[...]).astype(BF16)
        hp = _rms(xp_ref[...], g_ref[...])
        h_scr[0:FFN_HALO, :] = jnp.where(i % tiles_per_seq == 0, 0.0, hp).astype(BF16)
        o_ref[...] = x

    h = h_scr[...]

    def conv(u, c_ref):
        cw = c_ref[...]
        return (cw[2:3] * u[FFN_HALO:] + cw[1:2] * pltpu.roll(u, 1, axis=0)[FFN_HALO:]
                + cw[0:1] * pltpu.roll(u, 2, axis=0)[FFN_HALO:])

    a = conv(_dot(h, wa_ref[...]), ca_ref)
    b = conv(_dot(h, wb_ref[...]), cb_ref)
    o_ref[...] += _dot((_silu(a) * b).astype(BF16), wd_ref[...])


def _ffn(x, gain, w_up, conv_w, w_down, seq, tm=512, tf=1408):
    m, d = x.shape
    nf = D_FF // tf
    return pl.pallas_call(
        functools.partial(_ffn_kernel, tiles_per_seq=seq // tm),
        grid=(m // tm, nf),
        in_specs=[pl.BlockSpec((tm, d), lambda i, j: (i, 0)),
                  pl.BlockSpec((FFN_HALO, d),
                               lambda i, j: (jnp.maximum(i * (tm // FFN_HALO) - 1, 0), 0)),
                  pl.BlockSpec((1, d), lambda i, j: (0, 0)),
                  pl.BlockSpec((d, tf), lambda i, j: (0, j)),
                  pl.BlockSpec((d, tf), lambda i, j: (0, j + nf)),
                  pl.BlockSpec((3, tf), lambda i, j: (0, j)),
                  pl.BlockSpec((3, tf), lambda i, j: (0, j + nf)),
                  pl.BlockSpec((tf, d), lambda i, j: (j, 0))],
        out_specs=pl.BlockSpec((tm, d), lambda i, j: (i, 0)),
        out_shape=jax.ShapeDtypeStruct((m, d), F32),
        scratch_shapes=[pltpu.VMEM((tm + FFN_HALO, d), BF16)],
        compiler_params=_cparams("parallel", "arbitrary"),
        name="conv_ffn",
    )(x, x, gain, w_up, w_up, conv_w, conv_w, w_down)


def _split_w_in(w):
    o = 0
    q = w[:, o:o + NSA_W]; o += NSA_W
    kv = w[:, o:o + 6 * KV_W]; o += 6 * KV_W
    n_gate = NSA_HEADS * 3
    gl = w[:, o:o + n_gate]; o += n_gate
    rw = w[:, o:o + RW_COLS]; o += RW_COLS
    mg = w[:, o:]
    per_g = n_gate // NSA_KV_HEADS
    pad = jnp.zeros((w.shape[0], LANES - per_g), w.dtype)
    gates = [jnp.concatenate([gl[:, g * per_g:(g + 1) * per_g], pad], axis=1)
             for g in range(NSA_KV_HEADS)]
    w_nsa = jnp.concatenate([q, kv] + gates, axis=1)
    return w_nsa.astype(BF16), rw.astype(BF16), mg.astype(BF16)


def kernel(x, norm_mix, norm_ffn, w_in, qk_gain, cmp_pe, cmp_w1, cmp_w2, rwkv_mu, rwkv_w0, rwkv_w_up,
           rwkv_a0, rwkv_a_up, rwkv_g_up, rwkv_k_k, rwkv_k_a, rwkv_r_k, rwkv_ln_w, rwkv_ln_b, vres_v0,
           vres_v1, vres_v2, proj_nsa, proj_rwkv, w_out, ffn_up, ffn_conv, ffn_down):
    batch, seq, d = x.shape
    depth = w_in.shape[0]
    m = batch * seq
    assert seq % SLC_KV_TILE == 0 and seq // SLC_BLOCK <= LANES and seq >= WINDOW + Q_TILE
    xf = x.reshape(m, d)
    half = CMP_STRIDE * HEAD_DIM
    hi = lax.broadcasted_iota(jnp.int32, (RW_W, RW_W), 0) // HEAD_DIM
    hj = lax.broadcasted_iota(jnp.int32, (RW_W, RW_W), 1) // HEAD_DIM
    seg = (hi == hj).astype(F32)
    v_first = None
    for l in range(depth):
        w_nsa, w_rw, w_mg = _split_w_in(w_in[l])
        u_nsa, u_rw, u_mg = _inproj(xf, norm_mix[l][None], w_nsa, w_rw, w_mg)

        cmp_in, ksn, vs, kwn, vw = _nsa_prep(u_nsa, qk_gain[l])
        z = cmp_in.reshape(2, NSA_KV_HEADS, batch, seq // CMP_STRIDE, half)
        w1 = cmp_w1[l]
        w1ab = jnp.concatenate([w1[:, :CMP_STRIDE].reshape(2, half, HEAD_DIM),
                                w1[:, CMP_STRIDE:].reshape(2, half, HEAD_DIM)], axis=2).astype(BF16)
        w1flat = w1.reshape(2, 2 * half, HEAD_DIM).astype(BF16)
        pe8 = jnp.broadcast_to(cmp_pe[l].reshape(2, 1, 2 * half), (2, 8, 2 * half))
        cmp_kv = _nsa_compress(z, w1ab, pe8, w1flat, cmp_w2[l].astype(BF16), qk_gain[l])
        o_nsa = _nsa_attn(u_nsa, cmp_kv, ksn, vs, kwn, vw, qk_gain[l], batch, seq)

        zero = jnp.zeros((RW_W,), F32)
        vec = jnp.stack([rwkv_w0[l], rwkv_a0[l], rwkv_k_k[l], rwkv_k_a[l], rwkv_r_k[l].reshape(RW_W),
                         zero, zero, zero])
        vres = None
        if l > 0:
            vres = (v_first, vres_v0[l - 1][None], vres_v1[l - 1].astype(BF16),
                    vres_v2[l - 1].astype(BF16))
        r, lw, kh, v, kk, a, g, bonus = _rwkv_prep(
            u_rw, rwkv_mu[l][None], vec, rwkv_w_up[l].astype(BF16), rwkv_a_up[l].astype(BF16),
            rwkv_g_up[l].astype(BF16), seg, vres, seq)
        if l == 0:
            v_first = v
        o_rw = _rwkv_chunk(r, lw, kh, v, kk, a, g, bonus, rwkv_ln_w[l][None], rwkv_ln_b[l][None],
                           batch, seq, None)

        xf = _merge(xf, o_nsa, o_rw, u_mg, proj_nsa[l].astype(BF16), proj_rwkv[l].astype(BF16),
                    w_out[l].astype(BF16))
        xf = _ffn(xf, norm_ffn[l][None], ffn_up[l].astype(BF16), ffn_conv[l],
                  ffn_down[l].astype(BF16), seq)
    return xf.reshape(batch, seq, d)
```

```python
import functools

import jax
import jax.numpy as jnp
from jax import lax
from jax.experimental import pallas as pl
from jax.experimental.pallas import tpu as pltpu

F32 = jnp.float32
BF16 = jnp.bfloat16
HIGHEST = lax.Precision.HIGHEST

LANES = 128
VMEM_LIMIT_BYTES = 56 * 1024 * 1024

NSA_HEADS = 8
NSA_KV_HEADS = 2
NSA_GROUP = NSA_HEADS // NSA_KV_HEADS
HEAD_DIM = 64
CMP_BLOCK = 32
CMP_STRIDE = 16
SLC_BLOCK = 64
SLC_SHIFT = 6
SLC_TOPK = 16
WINDOW = 512
FORCE_SCORE = 1e4
NEG_INF = -1e30
LOG2_E = 1.4426950408889634
RWKV_HEADS = 8
DECAY_RANK = 64
ICLR_RANK = 64
GATE_RANK = 128
VRES_RANK = 32
LNX_EPS = 1e-5 * HEAD_DIM
D_FF = 2816
EPS = 1e-6

NSA_W = NSA_HEADS * HEAD_DIM
KV_W = NSA_KV_HEADS * HEAD_DIM
RW_W = RWKV_HEADS * HEAD_DIM
RW_COLS = 3 * RW_W + DECAY_RANK + ICLR_RANK + GATE_RANK
NSA_SLAB = NSA_W + 6 * KV_W + NSA_KV_HEADS * LANES

Q_TILE = 128
SLC_KV_TILE = 1024
RW_CHUNK = 64
RW_SUB = 16


def _cparams(*sem):
    return pltpu.CompilerParams(dimension_semantics=sem, vmem_limit_bytes=VMEM_LIMIT_BYTES)


def _dot(a, b, precision=None):
    return lax.dot_general(a, b, (((1,), (0,)), ((), ())), precision=precision,
                           preferred_element_type=F32)


def _dot_nt(a, b, precision=None):
    return lax.dot_general(a, b, (((1,), (1,)), ((), ())), precision=precision,
                           preferred_element_type=F32)


def _dot_tn(a, b, precision=None):
    return lax.dot_general(a, b, (((0,), (0,)), ((), ())), precision=precision,
                           preferred_element_type=F32)


def _rms(x, gain):
    return x * lax.rsqrt(jnp.mean(x * x, axis=-1, keepdims=True) + EPS) * gain


def _sigmoid(x):
    return 1.0 / (1.0 + jnp.exp(-x))


def _silu(x):
    return x * _sigmoid(x)


def _inproj_kernel(x_ref, g_ref, wn_ref, wr_ref, wm_ref, on_ref, or_ref, om_ref):
    h = _rms(x_ref[...], g_ref[...]).astype(BF16)
    on_ref[...] = _dot(h, wn_ref[...])
    or_ref[...] = _dot(h, wr_ref[...])
    om_ref[...] = _dot(h, wm_ref[...])


def _inproj(x, gain, w_nsa, w_rw, w_mg, tm=256):
    m, d = x.shape
    outs = [w_nsa.shape[1], w_rw.shape[1], w_mg.shape[1]]
    full = lambda i: (0, 0)
    return pl.pallas_call(
        _inproj_kernel,
        grid=(m // tm,),
        in_specs=[pl.BlockSpec((tm, d), lambda i: (i, 0)), pl.BlockSpec((1, d), full)]
        + [pl.BlockSpec((d, n), full) for n in outs],
        out_specs=[pl.BlockSpec((tm, n), lambda i: (i, 0)) for n in outs],
        out_shape=[jax.ShapeDtypeStruct((m, n), F32) for n in outs],
        compiler_params=_cparams("parallel"),
        name="inproj",
    )(x, gain, w_nsa, w_rw, w_mg)


MASK_BIG = 2.0 ** 99
KS_AUG_W = LANES + HEAD_DIM


def _nsa_prep_kernel(kc_ref, vc_ref, ks_ref, vs_ref, kw_ref, vw_ref, gn_ref, cmp_ref, ksa_ref,
                     vs_o_ref, kwn_ref, vw_o_ref, *, tiles_per_seq):
    gn = gn_ref[...]
    tm = ks_ref.shape[0]
    pos = (pl.program_id(0) % tiles_per_seq) * tm + lax.broadcasted_iota(jnp.int32, (tm, LANES), 0)
    blk_col = lax.broadcasted_iota(jnp.int32, (tm, LANES), 1)
    onehot = jnp.where(jnp.right_shift(pos, SLC_SHIFT) == blk_col, -MASK_BIG, 0.0).astype(BF16)
    for g in range(NSA_KV_HEADS):
        sl = slice(g * HEAD_DIM, (g + 1) * HEAD_DIM)
        cmp_ref[0, g] = kc_ref[:, sl]
        cmp_ref[1, g] = vc_ref[:, sl]
        ksa_ref[g, :, :LANES] = onehot
        ksa_ref[g, :, LANES:] = _rms(ks_ref[:, sl], gn[2:3]).astype(BF16)
        kwn_ref[g] = _rms(kw_ref[:, sl], gn[3:4]).astype(BF16)
        vs_o_ref[g] = vs_ref[:, sl].astype(BF16)
        vw_o_ref[g] = vw_ref[:, sl].astype(BF16)


def _nsa_prep(u_nsa, qk_gain, seq, tm=512):
    m = u_nsa.shape[0]
    g = NSA_KV_HEADS
    col0 = NSA_W // KV_W
    in_specs = [pl.BlockSpec((tm, KV_W), functools.partial(lambda i, c: (i, c), c=col0 + c))
                for c in range(6)]
    in_specs.append(pl.BlockSpec((4, HEAD_DIM), lambda i: (0, 0)))
    kv_spec = pl.BlockSpec((g, tm, HEAD_DIM), lambda i: (0, i, 0))
    kv_shape = jax.ShapeDtypeStruct((g, m, HEAD_DIM), BF16)
    return pl.pallas_call(
        functools.partial(_nsa_prep_kernel, tiles_per_seq=seq // tm),
        grid=(m // tm,),
        in_specs=in_specs,
        out_specs=[pl.BlockSpec((2, g, tm, HEAD_DIM), lambda i: (0, 0, i, 0)),
                   pl.BlockSpec((g, tm, KS_AUG_W), lambda i: (0, i, 0)), kv_spec, kv_spec, kv_spec],
        out_shape=[jax.ShapeDtypeStruct((2, g, m, HEAD_DIM), F32),
                   jax.ShapeDtypeStruct((g, m, KS_AUG_W), BF16), kv_shape, kv_shape, kv_shape],
        compiler_params=_cparams("parallel"),
        name="nsa_prep",
    )(*([u_nsa] * 6), qk_gain)


def _nsa_compress_kernel(z_ref, w1ab_ref, pe_ref, w1_ref, w2_ref, gn_ref, o_ref):
    kv = pl.program_id(0)
    nhalf = z_ref.shape[0]
    p = _dot(z_ref[...].astype(BF16), w1ab_ref[...])
    a = p[:, :HEAD_DIM]
    b_next = pltpu.roll(p[:, HEAD_DIM:], nhalf - 1, axis=0)
    c = _dot(pe_ref[...].astype(BF16), w1_ref[...])[0:1]
    hid = _silu(a + b_next + c)
    o = _dot(hid.astype(BF16), w2_ref[...])
    o_ref[...] = jnp.where(kv == 0, _rms(o, gn_ref[1:2]), o)


def _nsa_compress(z, w1ab, pe8, w1flat, w2, qk_gain):
    _, g, b, nhalf, zw = z.shape
    sq = lambda *shape: pl.BlockSpec((None,) + shape, lambda kv, gi, bi: (kv,) + (0,) * len(shape))
    return pl.pallas_call(
        _nsa_compress_kernel,
        grid=(2, g, b),
        in_specs=[pl.BlockSpec((None, None, None, nhalf, zw), lambda kv, gi, bi: (kv, gi, bi, 0, 0)),
                  sq(zw, 2 * HEAD_DIM), sq(8, 2 * zw), sq(2 * zw, HEAD_DIM), sq(HEAD_DIM, HEAD_DIM),
                  pl.BlockSpec((4, HEAD_DIM), lambda kv, gi, bi: (0, 0))],
        out_specs=pl.BlockSpec((None, None, None, nhalf, HEAD_DIM),
                               lambda kv, gi, bi: (kv, gi, bi, 0, 0)),
        out_shape=jax.ShapeDtypeStruct((2, g, b, nhalf, HEAD_DIM), F32),
        compiler_params=_cparams("parallel", "parallel", "parallel"),
        name="nsa_compress",
    )(z, w1ab, pe8, w1flat, w2, qk_gain)


def _masked_exp2_rows(s, mask):
    s = jnp.where(mask, s, NEG_INF)
    mx = jnp.max(s, axis=-1, keepdims=True)
    e = jnp.exp2(s - mx)
    inv = jnp.where(mx > 0.5 * NEG_INF, 1.0 / jnp.sum(e, axis=-1, keepdims=True), 0.0)
    return e, inv


def _nsa_attn_kernel(q_ref, gl_ref, kc_ref, vc_ref, ks_ref, vs_ref, kw_ref, vw_ref, gn_ref, o_ref,
                     *, seq):
    i = pl.program_id(2)
    tq = q_ref.shape[0]
    hg = NSA_GROUP
    t0 = i * tq
    scale = HEAD_DIM ** -0.5 * LOG2_E
    qg = gn_ref[0:1]
    qb = q_ref[...]
    q = jnp.concatenate(
        [_rms(qb[:, h * HEAD_DIM:(h + 1) * HEAD_DIM], qg) * scale for h in range(hg)],
        axis=0).astype(BF16)
    t_tok = t0 + lax.broadcasted_iota(jnp.int32, (tq, 1), 0)
    t_rows = jnp.concatenate([t_tok] * hg, axis=0)

    ncp = kc_ref.shape[0]
    s = _dot_nt(q, kc_ref[...].astype(BF16))
    n_idx = lax.broadcasted_iota(jnp.int32, (1, ncp), 1)
    cmp_valid = (n_idx * CMP_STRIDE + (CMP_BLOCK - 1)) <= t_rows
    e_cmp, inv_cmp = _masked_exp2_rows(s, cmp_valid)
    o_cmp = _dot(e_cmp.astype(BF16), vc_ref[...].astype(BF16)) * inv_cmp

    nsp = LANES
    p_sum = e_cmp[0:tq] * inv_cmp[0:tq]
    for h in range(1, hg):
        p_sum = p_sum + e_cmp[h * tq:(h + 1) * tq] * inv_cmp[h * tq:(h + 1) * tq]
    ov_n = lax.broadcasted_iota(jnp.int32, (ncp, nsp), 0) * CMP_STRIDE
    ov_s = lax.broadcasted_iota(jnp.int32, (ncp, nsp), 1) * SLC_BLOCK
    overlap = jnp.where((ov_n < ov_s + SLC_BLOCK) & (ov_n + CMP_BLOCK > ov_s), 1.0, 0.0).astype(BF16)
    p_hi = p_sum.astype(BF16)
    r1 = p_sum - p_hi.astype(F32)
    p_mid = r1.astype(BF16)
    p_lo = (r1 - p_mid.astype(F32)).astype(BF16)
    imp3 = _dot(jnp.concatenate([p_hi, p_mid, p_lo], axis=0), overlap)
    imp = imp3[0:tq] + imp3[tq:2 * tq] + imp3[2 * tq:3 * tq]
    s_idx = lax.broadcasted_iota(jnp.int32, (1, nsp), 1)
    cur = jnp.right_shift(t_tok, SLC_SHIFT)
    forced = (s_idx == 0) | (s_idx == cur) | (s_idx == cur - 1)
    imp = jnp.where(forced, FORCE_SCORE, imp)
    imp = jnp.where(s_idx > cur, -1.0, imp)

    span = WINDOW + tq
    w0 = pl.multiple_of(jnp.maximum(t0 - WINDOW, 0), tq)
    kw = kw_ref[pl.ds(w0, span), :]
    vw = vw_ref[pl.ds(w0, span), :]
    sw = _dot_nt(q, kw)
    dist = t_rows - (w0 + lax.broadcasted_iota(jnp.int32, (1, span), 1))
    e_win, inv_win = _masked_exp2_rows(sw, (dist >= 0) & (dist < WINDOW))
    o_win = _dot(e_win.astype(BF16), vw) * inv_win

    work = imp.T
    blk = lax.broadcasted_iota(jnp.int32, (nsp, tq), 0).astype(F32)
    unsel_t = jnp.ones((nsp, tq), F32)
    for _ in range(SLC_TOPK):
        mx = jnp.max(work, axis=0, keepdims=True)
        first = jnp.min(jnp.where(work == mx, blk, float(nsp)), axis=0, keepdims=True)
        chosen = blk == first
        unsel_t = jnp.where(chosen, 0.0, unsel_t)
        work = jnp.where(chosen, -3e38, work)
    unsel = unsel_t.T.astype(BF16)

    tk = SLC_KV_TILE
    q_aug = jnp.concatenate([jnp.concatenate([unsel] * hg, axis=0), q], axis=1)

    def slc_tile(j, carry, causal):
        k0 = pl.multiple_of(j * tk, tk)
        sc = _dot_nt(q_aug, ks_ref[pl.ds(k0, tk), :])
        if causal:
            sc = jnp.where(k0 + lax.broadcasted_iota(jnp.int32, (1, tk), 1) <= t_rows, sc, NEG_INF)
        m_i, l_i, acc = carry
        m_new = jnp.maximum(m_i, jnp.max(sc, axis=-1, keepdims=True))
        alpha = jnp.exp2(m_i - m_new)
        p = jnp.exp2(sc - m_new)
        l_new = alpha * l_i + jnp.sum(p, axis=-1, keepdims=True)
        return m_new, l_new, alpha * acc + _dot(p.astype(BF16), vs_ref[pl.ds(k0, tk), :])

    init = (jnp.full((hg * tq, 1), NEG_INF, F32), jnp.zeros((hg * tq, 1), F32),
            jnp.zeros((hg * tq, HEAD_DIM), F32))
    j_diag = t0 // tk
    carry = lax.fori_loop(0, j_diag, lambda j, c: slc_tile(j, c, False), init)
    _, l_s, acc_s = slc_tile(j_diag, carry, True)
    o_slc = acc_s / l_s

    gates = _sigmoid(gl_ref[...])
    outs = []
    for h in range(hg):
        r = slice(h * tq, (h + 1) * tq)
        outs.append(gates[:, 3 * h:3 * h + 1] * o_cmp[r] + gates[:, 3 * h + 1:3 * h + 2] * o_slc[r]
                    + gates[:, 3 * h + 2:3 * h + 3] * o_win[r])
    o_ref[...] = jnp.concatenate(outs, axis=1)


def _nsa_attn(u_nsa, cmp_kv, ksn, vs, kwn, vw, qk_gain, batch, seq):
    m = u_nsa.shape[0]
    g = NSA_KV_HEADS
    nq = seq // Q_TILE
    ncp = cmp_kv.shape[3]
    gw = NSA_GROUP * HEAD_DIM
    gate_col0 = (NSA_W + 6 * KV_W) // LANES
    cmp_spec = lambda kv: pl.BlockSpec((None, None, None, ncp, HEAD_DIM),
                                       lambda b, gi, i: (kv, gi, b, 0, 0))
    kv_spec = pl.BlockSpec((None, seq, HEAD_DIM), lambda b, gi, i: (gi, b, 0))
    ksa_spec = pl.BlockSpec((None, seq, KS_AUG_W), lambda b, gi, i: (gi, b, 0))
    return pl.pallas_call(
        functools.partial(_nsa_attn_kernel, seq=seq),
        grid=(batch, g, nq),
        in_specs=[pl.BlockSpec((Q_TILE, gw), lambda b, gi, i: (b * nq + i, gi)),
                  pl.BlockSpec((Q_TILE, LANES), lambda b, gi, i: (b * nq + i, gate_col0 + gi)),
                  cmp_spec(0), cmp_spec(1), ksa_spec, kv_spec, kv_spec, kv_spec,
                  pl.BlockSpec((4, HEAD_DIM), lambda b, gi, i: (0, 0))],
        out_specs=pl.BlockSpec((Q_TILE, gw), lambda b, gi, i: (b * nq + i, gi)),
        out_shape=jax.ShapeDtypeStruct((m, NSA_W), F32),
        compiler_params=_cparams("parallel", "parallel", "arbitrary"),
        name="nsa_attn",
    )(u_nsa, u_nsa, cmp_kv, cmp_kv, ksn, vs, kwn, vw, qk_gain)


def _rwkv_prep_kernel(*refs, tiles_per_seq, has_vres):
    if has_vres:
        (z_ref, zp_ref, mu_ref, vec_ref, wup_ref, aup_ref, gup_ref, seg_ref, vf_ref, v0_ref, v1_ref,
         v2_ref, r_o, lw_o, k_o, v_o, kk_o, a_o, g_o, bonus_o) = refs
    else:
        (z_ref, zp_ref, mu_ref, vec_ref, wup_ref, aup_ref, gup_ref, seg_ref,
         r_o, lw_o, k_o, v_o, kk_o, a_o, g_o, bonus_o) = refs
    i = pl.program_id(0)
    z = z_ref[...]
    tm = z.shape[0]
    prev_last = jnp.where(i % tiles_per_seq == 0, 0.0, zp_ref[7:8, :])
    row = lax.broadcasted_iota(jnp.int32, (tm, 1), 0)
    z_prev = jnp.where(row == 0, prev_last, pltpu.roll(z, 1, axis=0))
    z = z + mu_ref[...] * (z_prev - z)
    w = RW_W
    r = z[:, 0:w]
    k = z[:, w:2 * w]
    v = z[:, 2 * w:3 * w]
    wd = z[:, 3 * w:3 * w + DECAY_RANK]
    ad = z[:, 3 * w + DECAY_RANK:3 * w + DECAY_RANK + ICLR_RANK]
    gd = z[:, 3 * w + DECAY_RANK + ICLR_RANK:]
    vec = vec_ref[...]
    w0, a0, k_k, k_a, r_k = (vec[n:n + 1] for n in range(5))
    if has_vres:
        lo = _dot(_dot(v.astype(BF16), v1_ref[...]).astype(BF16), v2_ref[...])
        v = v + (vf_ref[...] - v) * _sigmoid(v0_ref[...] + lo)
    wl = w0 + _dot(jnp.tanh(wd).astype(BF16), wup_ref[...])
    neg = -wl
    softplus = jnp.maximum(neg, 0.0) + jnp.log(1.0 + jnp.exp(-jnp.abs(neg)))
    lw_o[...] = -jnp.exp(-softplus - 0.5)
    a = _sigmoid(a0 + _dot(ad.astype(BF16), aup_ref[...]))
    g_o[...] = _dot(_sigmoid(gd).astype(BF16), gup_ref[...])
    seg = seg_ref[...]
    kk = k * k_k
    nrm = jnp.sqrt(_dot(kk * kk, seg, HIGHEST))
    kk_o[...] = kk / jnp.maximum(nrm, 1e-12)
    kh = k * (1.0 + (a - 1.0) * k_a)
    bonus_o[...] = _dot(r * kh * r_k, seg, HIGHEST) * v
    r_o[...] = r
    k_o[...] = kh
    v_o[...] = v
    a_o[...] = a


def _rwkv_prep(u_rw, mu, vec, wup, aup, gup, seg, vres, seq, tm=256):
    m = u_rw.shape[0]
    full = lambda i: (0, 0)
    row = lambda i: (i, 0)
    in_specs = [pl.BlockSpec((tm, RW_COLS), row),
                pl.BlockSpec((8, RW_COLS), lambda i: (jnp.maximum(i * (tm // 8) - 1, 0), 0)),
                pl.BlockSpec((1, RW_COLS), full), pl.BlockSpec((8, RW_W), full),
                pl.BlockSpec((DECAY_RANK, RW_W), full), pl.BlockSpec((ICLR_RANK, RW_W), full),
                pl.BlockSpec((GATE_RANK, RW_W), full), pl.BlockSpec((RW_W, RW_W), full)]
    args = [u_rw, u_rw, mu, vec, wup, aup, gup, seg]
    if vres is not None:
        v_first, v0, v1, v2 = vres
        in_specs += [pl.BlockSpec((tm, RW_W), row), pl.BlockSpec((1, RW_W), full),
                     pl.BlockSpec((RW_W, VRES_RANK), full), pl.BlockSpec((VRES_RANK, RW_W), full)]
        args += [v_first, v0, v1, v2]
    return pl.pallas_call(
        functools.partial(_rwkv_prep_kernel, tiles_per_seq=seq // tm, has_vres=vres is not None),
        grid=(m // tm,),
        in_specs=in_specs,
        out_specs=[pl.BlockSpec((tm, RW_W), row)] * 8,
        out_shape=[jax.ShapeDtypeStruct((m, RW_W), F32)] * 8,
        compiler_params=_cparams("parallel"),
        name="rwkv_prep",
    )(*args)


def _b(x):
    return x.astype(BF16)


def _tri_inverse(l_list, eye):
    c = eye.shape[0]
    ri = lax.broadcasted_iota(jnp.int32, (c, c), 0)
    ci = lax.broadcasted_iota(jnp.int32, (c, c), 1)
    same_blk = (ri // RW_SUB) == (ci // RW_SUB)
    ld = [jnp.where(same_blk, l, 0.0) for l in l_list]
    lo = [_b(l - d0) for l, d0 in zip(l_list, ld)]
    d = [eye + x for x in ld]
    p = [_b(x) for x in ld]
    sq = 1
    while 2 * sq < RW_SUB:
        p = [_b(_dot(x, x)) for x in p]
        d = [x + _dot(_b(x), y) for x, y in zip(d, p)]
        sq *= 2
    d = [_b(x) for x in d]
    n = [_dot(x, y) for x, y in zip(d, lo)]
    x = [eye + y for y in n]
    n = [_b(y) for y in n]
    terms = 2
    while terms < c // RW_SUB:
        n = [_b(_dot(y, y)) for y in n]
        x = [z + _dot(_b(z), y) for z, y in zip(x, n)]
        terms *= 2
    return [_dot(_b(z), y) for z, y in zip(x, d)]


def _rwkv_chunk_kernel(r_ref, lw_ref, k_ref, v_ref, kk_ref, a_ref, g_ref, bonus_ref, lnw_ref,
                       lnb_ref, o_ref, state_ref, *, chain_prec):
    t = pl.program_id(2)

    @pl.when(t == 0)
    def _():
        state_ref[...] = jnp.zeros_like(state_ref)

    rt_rows = r_ref.shape[0]
    c = RW_CHUNK
    hd = HEAD_DIM
    nh = LANES // hd
    ri = lax.broadcasted_iota(jnp.int32, (c, c), 0)
    ci = lax.broadcasted_iota(jnp.int32, (c, c), 1)
    tril_incl = ri >= ci
    tril_strict = ri > ci
    is_diag = ri == ci
    ones_incl = jnp.where(tril_incl, 1.0, 0.0).astype(BF16)
    eye = jnp.where(is_diag, 1.0, 0.0)
    if chain_prec is None:
        chain = lambda x: _b(x)
    else:
        chain = lambda x: x

    nc = rt_rows // c
    rows = [slice(cb * c, (cb + 1) * c) for cb in range(nc)]
    lanes = [slice(hh * hd, (hh + 1) * hd) for hh in range(nh)]
    items = [(cb, hh) for cb in range(nc) for hh in range(nh)]

    def cumulative(lw):
        lw_hi = _b(lw)
        lw_r = lw - lw_hi.astype(F32)
        lw_mid = _b(lw_r)
        lw_lo = _b(lw_r - lw_mid.astype(F32))
        g3 = _dot(ones_incl, jnp.concatenate([lw_hi, lw_mid, lw_lo], axis=1))
        return g3[:, :LANES] + g3[:, LANES:2 * LANES] + g3[:, 2 * LANES:]

    lw = [lw_ref[rs, :] for rs in rows]
    gcum = [cumulative(x) for x in lw]
    g_last = [x[c - 1:c] for x in gcum]
    kk = [kk_ref[rs, :] for rs in rows]
    k = [k_ref[rs, :] for rs in rows]
    v2 = [v_ref[rs, :] for rs in rows]
    b = [kk[cb] * a_ref[rows[cb], :] for cb in range(nc)]
    e_neg = [jnp.exp(-x) for x in gcum]
    e_end = [jnp.exp(g_last[cb] - gcum[cb]) for cb in range(nc)]
    a_t2 = [(-kk[cb]) * jnp.exp(gcum[cb] - lw[cb]) for cb in range(nc)]
    r_t2 = [r_ref[rows[cb], :] * jnp.exp(gcum[cb]) for cb in range(nc)]
    b_t2 = [b[cb] * e_neg[cb] for cb in range(nc)]
    k_t2 = [k[cb] * e_neg[cb] for cb in range(nc)]
    b_end2 = [b[cb] * e_end[cb] for cb in range(nc)]
    k_end2 = [k[cb] * e_end[cb] for cb in range(nc)]
    decay2 = [jnp.exp(x) for x in g_last]

    gram = [_dot_nt(_b(jnp.concatenate([a_t2[cb][:, lanes[hh]], r_t2[cb][:, lanes[hh]]], axis=0)),
                    _b(jnp.concatenate([b_t2[cb][:, lanes[hh]], k_t2[cb][:, lanes[hh]]], axis=0)))
            for cb, hh in items]
    l_ab = [jnp.where(tril_strict, x[:c, :c], 0.0) for x in gram]
    m_rb = [_b(jnp.where(tril_incl, x[c:, :c], 0.0)) for x in gram]
    lm = [_b(jnp.concatenate([jnp.where(tril_strict, x[:c, c:], 0.0),
                              jnp.where(tril_incl, x[c:, c:], 0.0)], axis=0)) for x in gram]
    v_b = [_b(v2[cb][:, lanes[hh]]) for cb, hh in items]
    lmv = [_dot(x, y) for x, y in zip(lm, v_b)]
    tinv = _tri_inverse(l_ab, eye)
    tw = [_dot(_b(tinv[i]), _b(jnp.concatenate([a_t2[cb][:, lanes[hh]], lmv[i][:c]], axis=1)))
          for i, (cb, hh) in enumerate(items)]
    tw_b = [_b(x) for x in tw]
    mb = [_dot(x, y) for x, y in zip(m_rb, tw_b)]
    q_eff = [chain(r_t2[cb][:, lanes[hh]] + mb[i][:, :hd]) for i, (cb, hh) in enumerate(items)]
    y_loc = [mb[i][:, hd:] + lmv[i][c:] for i in range(len(items))]
    a_c = [chain(jnp.where(is_diag, decay2[cb][:, lanes[hh]], 0.0)
                 + _dot_tn(tw_b[i][:, :hd], _b(b_end2[cb][:, lanes[hh]])))
           for i, (cb, hh) in enumerate(items)]
    b_c = [_dot_tn(jnp.concatenate([tw_b[i][:, hd:], v_b[i]], axis=0),
                   _b(jnp.concatenate([b_end2[cb][:, lanes[hh]], k_end2[cb][:, lanes[hh]]], axis=0)))
           for i, (cb, hh) in enumerate(items)]

    states = [state_ref[hh] for hh in range(nh)]
    for i, (cb, hh) in enumerate(items):
        s0 = chain(states[hh])
        y = _dot_nt(q_eff[i], s0, chain_prec) + y_loc[i]
        states[hh] = _dot(s0, a_c[i], chain_prec) + b_c[i]
        rs, ls = rows[cb], lanes[hh]
        mean = jnp.mean(y, axis=-1, keepdims=True)
        yc = y - mean
        var = jnp.mean(yc * yc, axis=-1, keepdims=True)
        yn = yc * lax.rsqrt(var + LNX_EPS) * lnw_ref[:, ls] + lnb_ref[:, ls]
        o_ref[rs, ls] = (yn + bonus_ref[rs, ls]) * g_ref[rs, ls]
    for hh in range(nh):
        state_ref[hh] = states[hh]


def _rwkv_chunk(r, lw, k, v, kk, a, g, bonus, lnw, lnb, batch, seq, chain_prec, rt=512):
    m = r.shape[0]
    nt = seq // rt
    pairs = RW_W // LANES
    row = pl.BlockSpec((rt, LANES), lambda b, p, t: (b * nt + t, p))
    vec = pl.BlockSpec((1, LANES), lambda b, p, t: (0, p))
    return pl.pallas_call(
        functools.partial(_rwkv_chunk_kernel, chain_prec=chain_prec),
        grid=(batch, pairs, nt),
        in_specs=[row] * 8 + [vec, vec],
        out_specs=row,
        out_shape=jax.ShapeDtypeStruct((m, RW_W), F32),
        scratch_shapes=[pltpu.VMEM((LANES // HEAD_DIM, HEAD_DIM, HEAD_DIM), F32)],
        compiler_params=_cparams("parallel", "parallel", "arbitrary"),
        name="rwkv_chunk",
    )(r, lw, k, v, kk, a, g, bonus, lnw, lnb)


def _merge_kernel(x_ref, on_ref, or_ref, mg_ref, pa_ref, pb_ref, wo_ref, o_ref):
    d = x_ref.shape[1]
    ya = _dot(on_ref[...].astype(BF16), pa_ref[...])
    yb = _dot(or_ref[...].astype(BF16), pb_ref[...])
    y = _sigmoid(mg_ref[:, :d]) * ya + _sigmoid(mg_ref[:, d:]) * yb
    o_ref[...] = x_ref[...] + _dot(y.astype(BF16), wo_ref[...])


def _merge(x, o_nsa, o_rw, u_mg, pa, pb, wo, tm=512):
    m, d = x.shape
    full = lambda i: (0, 0)
    row = lambda i: (i, 0)
    return pl.pallas_call(
        _merge_kernel,
        grid=(m // tm,),
        in_specs=[pl.BlockSpec((tm, d), row), pl.BlockSpec((tm, NSA_W), row),
                  pl.BlockSpec((tm, RW_W), row), pl.BlockSpec((tm, 2 * d), row),
                  pl.BlockSpec((NSA_W, d), full), pl.BlockSpec((RW_W, d), full),
                  pl.BlockSpec((d, d), full)],
        out_specs=pl.BlockSpec((tm, d), row),
        out_shape=jax.ShapeDtypeStruct((m, d), F32),
        compiler_params=_cparams("parallel"),
        name="merge_out",
    )(x, o_nsa, o_rw, u_mg, pa, pb, wo)


FFN_HALO = 16


def _ffn_kernel(x_ref, xp_ref, g_ref, wa_ref, wb_ref, ca_ref, cb_ref, wd_ref, o_ref, h_scr, *,
                tiles_per_seq):
    i = pl.program_id(0)
    j = pl.program_id(1)

    @pl.when(j == 0)
    def _():
        x = x_ref[...]
        h_scr[FFN_HALO:, :] = _rms(x, g_ref[...]).astype(BF16)
        hp = _rms(xp_ref[...], g_ref[...])
        h_scr[0:FFN_HALO, :] = jnp.where(i % tiles_per_seq == 0, 0.0, hp).astype(BF16)
        o_ref[...] = x

    h = h_scr[...]

    def conv(u, c_ref):
        cw = c_ref[...]
        return (cw[2:3] * u[FFN_HALO:] + cw[1:2] * pltpu.roll(u, 1, axis=0)[FFN_HALO:]
                + cw[0:1] * pltpu.roll(u, 2, axis=0)[FFN_HALO:])

    a = conv(_dot(h, wa_ref[...]), ca_ref)
    b = conv(_dot(h, wb_ref[...]), cb_ref)
    o_ref[...] += _dot((_silu(a) * b).astype(BF16), wd_ref[...])


def _ffn(x, gain, w_up, conv_w, w_down, seq, tm=512, tf=1408):
    m, d = x.shape
    nf = D_FF // tf
    return pl.pallas_call(
        functools.partial(_ffn_kernel, tiles_per_seq=seq // tm),
        grid=(m // tm, nf),
        in_specs=[pl.BlockSpec((tm, d), lambda i, j: (i, 0)),
                  pl.BlockSpec((FFN_HALO, d),
                               lambda i, j: (jnp.maximum(i * (tm // FFN_HALO) - 1, 0), 0)),
                  pl.BlockSpec((1, d), lambda i, j: (0, 0)),
                  pl.BlockSpec((d, tf), lambda i, j: (0, j)),
                  pl.BlockSpec((d, tf), lambda i, j: (0, j + nf)),
                  pl.BlockSpec((3, tf), lambda i, j: (0, j)),
                  pl.BlockSpec((3, tf), lambda i, j: (0, j + nf)),
                  pl.BlockSpec((tf, d), lambda i, j: (j, 0))],
        out_specs=pl.BlockSpec((tm, d), lambda i, j: (i, 0)),
        out_shape=jax.ShapeDtypeStruct((m, d), F32),
        scratch_shapes=[pltpu.VMEM((tm + FFN_HALO, d), BF16)],
        compiler_params=_cparams("parallel", "arbitrary"),
        name="conv_ffn",
    )(x, x, gain, w_up, w_up, conv_w, conv_w, w_down)


def _split_w_in(w):
    o = 0
    q = w[:, o:o + NSA_W]; o += NSA_W
    kv = w[:, o:o + 6 * KV_W]; o += 6 * KV_W
    n_gate = NSA_HEADS * 3
    gl = w[:, o:o + n_gate]; o += n_gate
    rw = w[:, o:o + RW_COLS]; o += RW_COLS
    mg = w[:, o:]
    per_g = n_gate // NSA_KV_HEADS
    pad = jnp.zeros((w.shape[0], LANES - per_g), w.dtype)
    gates = [jnp.concatenate([gl[:, g * per_g:(g + 1) * per_g], pad], axis=1)
             for g in range(NSA_KV_HEADS)]
    w_nsa = jnp.concatenate([q, kv] + gates, axis=1)
    return w_nsa.astype(BF16), rw.astype(BF16), mg.astype(BF16)


def kernel(x, norm_mix, norm_ffn, w_in, qk_gain, cmp_pe, cmp_w1, cmp_w2, rwkv_mu, rwkv_w0, rwkv_w_up,
           rwkv_a0, rwkv_a_up, rwkv_g_up, rwkv_k_k, rwkv_k_a, rwkv_r_k, rwkv_ln_w, rwkv_ln_b, vres_v0,
           vres_v1, vres_v2, proj_nsa, proj_rwkv, w_out, ffn_up, ffn_conv, ffn_down):
    batch, seq, d = x.shape
    depth = w_in.shape[0]
    m = batch * seq
    assert seq % SLC_KV_TILE == 0 and seq // SLC_BLOCK <= LANES and seq >= WINDOW + Q_TILE
    xf = x.reshape(m, d)
    half = CMP_STRIDE * HEAD_DIM
    hi = lax.broadcasted_iota(jnp.int32, (RW_W, RW_W), 0) // HEAD_DIM
    hj = lax.broadcasted_iota(jnp.int32, (RW_W, RW_W), 1) // HEAD_DIM
    seg = (hi == hj).astype(F32)
    v_first = None
    for l in range(depth):
        w_nsa, w_rw, w_mg = _split_w_in(w_in[l])
        u_nsa, u_rw, u_mg = _inproj(xf, norm_mix[l][None], w_nsa, w_rw, w_mg)

        cmp_in, ksn, vs, kwn, vw = _nsa_prep(u_nsa, qk_gain[l], seq)
        z = cmp_in.reshape(2, NSA_KV_HEADS, batch, seq // CMP_STRIDE, half)
        w1 = cmp_w1[l]
        w1ab = jnp.concatenate([w1[:, :CMP_STRIDE].reshape(2, half, HEAD_DIM),
                                w1[:, CMP_STRIDE:].reshape(2, half, HEAD_DIM)], axis=2).astype(BF16)
        w1flat = w1.reshape(2, 2 * half, HEAD_DIM).astype(BF16)
        pe8 = jnp.broadcast_to(cmp_pe[l].reshape(2, 1, 2 * half), (2, 8, 2 * half))
        cmp_kv = _nsa_compress(z, w1ab, pe8, w1flat, cmp_w2[l].astype(BF16), qk_gain[l])
        o_nsa = _nsa_attn(u_nsa, cmp_kv, ksn, vs, kwn, vw, qk_gain[l], batch, seq)

        zero = jnp.zeros((RW_W,), F32)
        vec = jnp.stack([rwkv_w0[l], rwkv_a0[l], rwkv_k_k[l], rwkv_k_a[l], rwkv_r_k[l].reshape(RW_W),
                         zero, zero, zero])
        vres = None
        if l > 0:
            vres = (v_first, vres_v0[l - 1][None], vres_v1[l - 1].astype(BF16),
                    vres_v2[l - 1].astype(BF16))
        r, lw, kh, v, kk, a, g, bonus = _rwkv_prep(
            u_rw, rwkv_mu[l][None], vec, rwkv_w_up[l].astype(BF16), rwkv_a_up[l].astype(BF16),
            rwkv_g_up[l].astype(BF16), seg, vres, seq)
        if l == 0:
            v_first = v
        o_rw = _rwkv_chunk(r, lw, kh, v, kk, a, g, bonus, rwkv_ln_w[l][None], rwkv_ln_b[l][None],
                           batch, seq, None)

        xf = _merge(xf, o_nsa, o_rw, u_mg, proj_nsa[l].astype(BF16), proj_rwkv[l].astype(BF16),
                    w_out[l].astype(BF16))
        xf = _ffn(xf, norm_ffn[l][None], ffn_up[l].astype(BF16), ffn_conv[l],
                  ffn_down[l].astype(BF16), seq)
    return xf.reshape(batch, seq, d)
```

```python
import functools

import jax
import jax.numpy as jnp
from jax import lax
from jax.experimental import pallas as pl
from jax.experimental.pallas import tpu as pltpu

F32 = jnp.float32
BF16 = jnp.bfloat16
HIGHEST = lax.Precision.HIGHEST

LANES = 128
VMEM_LIMIT_BYTES = 56 * 1024 * 1024

NSA_HEADS = 8
NSA_KV_HEADS = 2
NSA_GROUP = NSA_HEADS // NSA_KV_HEADS
HEAD_DIM = 64
CMP_BLOCK = 32
CMP_STRIDE = 16
SLC_BLOCK = 64
SLC_SHIFT = 6
SLC_TOPK = 16
WINDOW = 512
FORCE_SCORE = 1e4
NEG_INF = -1e30
LOG2_E = 1.4426950408889634
RWKV_HEADS = 8
DECAY_RANK = 64
ICLR_RANK = 64
GATE_RANK = 128
VRES_RANK = 32
LNX_EPS = 1e-5 * HEAD_DIM
D_FF = 2816
EPS = 1e-6

NSA_W = NSA_HEADS * HEAD_DIM
KV_W = NSA_KV_HEADS * HEAD_DIM
RW_W = RWKV_HEADS * HEAD_DIM
RW_COLS = 3 * RW_W + DECAY_RANK + ICLR_RANK + GATE_RANK
NSA_SLAB = NSA_W + 6 * KV_W + NSA_KV_HEADS * LANES

Q_TILE = 128
SLC_KV_TILE = 1024
RW_CHUNK = 64
RW_SUB = 16


def _cparams(*sem):
    return pltpu.CompilerParams(dimension_semantics=sem, vmem_limit_bytes=VMEM_LIMIT_BYTES)


def _dot(a, b, precision=None):
    return lax.dot_general(a, b, (((1,), (0,)), ((), ())), precision=precision,
                           preferred_element_type=F32)


def _dot_nt(a, b, precision=None):
    return lax.dot_general(a, b, (((1,), (1,)), ((), ())), precision=precision,
                           preferred_element_type=F32)


def _dot_tn(a, b, precision=None):
    return lax.dot_general(a, b, (((0,), (0,)), ((), ())), precision=precision,
                           preferred_element_type=F32)


def _rms(x, gain):
    return x * lax.rsqrt(jnp.mean(x * x, axis=-1, keepdims=True) + EPS) * gain


def _sigmoid(x):
    return 1.0 / (1.0 + jnp.exp(-x))


def _silu(x):
    return x * _sigmoid(x)


def _inproj_kernel(x_ref, g_ref, wn_ref, wr_ref, wm_ref, on_ref, or_ref, om_ref):
    h = _rms(x_ref[...], g_ref[...]).astype(BF16)
    on_ref[...] = _dot(h, wn_ref[...])
    or_ref[...] = _dot(h, wr_ref[...])
    om_ref[...] = _dot(h, wm_ref[...])


def _inproj(x, gain, w_nsa, w_rw, w_mg, tm=256):
    m, d = x.shape
    outs = [w_nsa.shape[1], w_rw.shape[1], w_mg.shape[1]]
    full = lambda i: (0, 0)
    return pl.pallas_call(
        _inproj_kernel,
        grid=(m // tm,),
        in_specs=[pl.BlockSpec((tm, d), lambda i: (i, 0)), pl.BlockSpec((1, d), full)]
        + [pl.BlockSpec((d, n), full) for n in outs],
        out_specs=[pl.BlockSpec((tm, n), lambda i: (i, 0)) for n in outs],
        out_shape=[jax.ShapeDtypeStruct((m, n), F32) for n in outs],
        compiler_params=_cparams("parallel"),
        name="inproj",
    )(x, gain, w_nsa, w_rw, w_mg)


MASK_BIG = 2.0 ** 99
KS_AUG_W = LANES + HEAD_DIM


V_AUG_W = 2 * HEAD_DIM


def _nsa_prep_kernel(q_ref, kc_ref, vc_ref, ks_ref, vs_ref, kw_ref, vw_ref, gn_ref, qn_ref, cmp_ref,
                     ksa_ref, vs_o_ref, kwn_ref, vw_o_ref, *, tiles_per_seq):
    gn = gn_ref[...]
    tm = ks_ref.shape[0]
    pos = (pl.program_id(0) % tiles_per_seq) * tm + lax.broadcasted_iota(jnp.int32, (tm, LANES), 0)
    blk_col = lax.broadcasted_iota(jnp.int32, (tm, LANES), 1)
    onehot = jnp.where(jnp.right_shift(pos, SLC_SHIFT) == blk_col, -MASK_BIG, 0.0).astype(BF16)
    ones = jnp.ones((tm, HEAD_DIM), BF16)
    q_scale = HEAD_DIM ** -0.5 * LOG2_E
    for h in range(NSA_HEADS):
        qh = q_ref[:, h * HEAD_DIM:(h + 1) * HEAD_DIM]
        qn_ref[h // NSA_GROUP, h % NSA_GROUP] = (_rms(qh, gn[0:1]) * q_scale).astype(BF16)
    for g in range(NSA_KV_HEADS):
        sl = slice(g * HEAD_DIM, (g + 1) * HEAD_DIM)
        cmp_ref[0, g] = kc_ref[:, sl]
        cmp_ref[1, g] = vc_ref[:, sl]
        ksa_ref[g, :, :LANES] = onehot
        ksa_ref[g, :, LANES:] = _rms(ks_ref[:, sl], gn[2:3]).astype(BF16)
        kwn_ref[g] = _rms(kw_ref[:, sl], gn[3:4]).astype(BF16)
        vs_o_ref[g, :, :HEAD_DIM] = vs_ref[:, sl].astype(BF16)
        vs_o_ref[g, :, HEAD_DIM:] = ones
        vw_o_ref[g, :, :HEAD_DIM] = vw_ref[:, sl].astype(BF16)
        vw_o_ref[g, :, HEAD_DIM:] = ones


def _nsa_prep(u_nsa, qk_gain, seq, tm=512):
    m = u_nsa.shape[0]
    g = NSA_KV_HEADS
    col0 = NSA_W // KV_W
    in_specs = [pl.BlockSpec((tm, NSA_W), lambda i: (i, 0))]
    in_specs += [pl.BlockSpec((tm, KV_W), functools.partial(lambda i, c: (i, c), c=col0 + c))
                 for c in range(6)]
    in_specs.append(pl.BlockSpec((4, HEAD_DIM), lambda i: (0, 0)))
    spec3 = lambda w: pl.BlockSpec((g, tm, w), lambda i: (0, i, 0))
    shape3 = lambda w: jax.ShapeDtypeStruct((g, m, w), BF16)
    return pl.pallas_call(
        functools.partial(_nsa_prep_kernel, tiles_per_seq=seq // tm),
        grid=(m // tm,),
        in_specs=in_specs,
        out_specs=[pl.BlockSpec((g, NSA_GROUP, tm, HEAD_DIM), lambda i: (0, 0, i, 0)),
                   pl.BlockSpec((2, g, tm, HEAD_DIM), lambda i: (0, 0, i, 0)),
                   spec3(KS_AUG_W), spec3(V_AUG_W), spec3(HEAD_DIM), spec3(V_AUG_W)],
        out_shape=[jax.ShapeDtypeStruct((g, NSA_GROUP, m, HEAD_DIM), BF16),
                   jax.ShapeDtypeStruct((2, g, m, HEAD_DIM), F32),
                   shape3(KS_AUG_W), shape3(V_AUG_W), shape3(HEAD_DIM), shape3(V_AUG_W)],
        compiler_params=_cparams("parallel"),
        name="nsa_prep",
    )(*([u_nsa] * 7), qk_gain)


def _nsa_compress_kernel(z_ref, w1ab_ref, pe_ref, w1_ref, w2_ref, gn_ref, o_ref):
    kv = pl.program_id(0)
    nhalf = z_ref.shape[0]
    p = _dot(z_ref[...].astype(BF16), w1ab_ref[...])
    a = p[:, :HEAD_DIM]
    b_next = pltpu.roll(p[:, HEAD_DIM:], nhalf - 1, axis=0)
    c = _dot(pe_ref[...].astype(BF16), w1_ref[...])[0:1]
    hid = _silu(a + b_next + c)
    o = _dot(hid.astype(BF16), w2_ref[...])
    o_ref[:, :HEAD_DIM] = jnp.where(kv == 0, _rms(o, gn_ref[1:2]), o).astype(BF16)
    o_ref[:, HEAD_DIM:] = jnp.ones((nhalf, HEAD_DIM), BF16)


def _nsa_compress(z, w1ab, pe8, w1flat, w2, qk_gain):
    _, g, b, nhalf, zw = z.shape
    sq = lambda *shape: pl.BlockSpec((None,) + shape, lambda kv, gi, bi: (kv,) + (0,) * len(shape))
    return pl.pallas_call(
        _nsa_compress_kernel,
        grid=(2, g, b),
        in_specs=[pl.BlockSpec((None, None, None, nhalf, zw), lambda kv, gi, bi: (kv, gi, bi, 0, 0)),
                  sq(zw, 2 * HEAD_DIM), sq(8, 2 * zw), sq(2 * zw, HEAD_DIM), sq(HEAD_DIM, HEAD_DIM),
                  pl.BlockSpec((4, HEAD_DIM), lambda kv, gi, bi: (0, 0))],
        out_specs=pl.BlockSpec((None, None, None, nhalf, V_AUG_W),
                               lambda kv, gi, bi: (kv, gi, bi, 0, 0)),
        out_shape=jax.ShapeDtypeStruct((2, g, b, nhalf, V_AUG_W), BF16),
        compiler_params=_cparams("parallel", "parallel", "parallel"),
        name="nsa_compress",
    )(z, w1ab, pe8, w1flat, w2, qk_gain)


def _masked_attend(s, mask, v_aug):
    s = jnp.where(mask, s, NEG_INF)
    mx = jnp.max(s, axis=-1, keepdims=True)
    e = jnp.exp2(s - mx)
    ov = _dot(e.astype(BF16), v_aug)
    inv = jnp.where(mx > 0.5 * NEG_INF, 1.0 / ov[:, HEAD_DIM:HEAD_DIM + 1], 0.0)
    return e, ov[:, :HEAD_DIM] * inv, inv


def _nsa_attn_kernel(q_ref, gl_ref, kc_ref, vc_ref, ks_ref, vs_ref, kw_ref, vw_ref, o_ref, *, seq):
    i = pl.program_id(2)
    hg, tq = q_ref.shape[0], q_ref.shape[1]
    t0 = i * tq
    q = q_ref[...].reshape(hg * tq, HEAD_DIM)
    t_tok = t0 + lax.broadcasted_iota(jnp.int32, (tq, 1), 0)
    t_rows = jnp.concatenate([t_tok] * hg, axis=0)

    ncp = kc_ref.shape[0]
    s = _dot_nt(q, kc_ref[:, :HEAD_DIM])
    n_idx = lax.broadcasted_iota(jnp.int32, (1, ncp), 1)
    cmp_valid = (n_idx * CMP_STRIDE + (CMP_BLOCK - 1)) <= t_rows
    e_cmp, o_cmp, inv_cmp = _masked_attend(s, cmp_valid, vc_ref[...])

    nsp = LANES
    p_sum = e_cmp[0:tq] * inv_cmp[0:tq]
    for h in range(1, hg):
        p_sum = p_sum + e_cmp[h * tq:(h + 1) * tq] * inv_cmp[h * tq:(h + 1) * tq]
    ov_n = lax.broadcasted_iota(jnp.int32, (ncp, nsp), 0) * CMP_STRIDE
    ov_s = lax.broadcasted_iota(jnp.int32, (ncp, nsp), 1) * SLC_BLOCK
    overlap = jnp.where((ov_n < ov_s + SLC_BLOCK) & (ov_n + CMP_BLOCK > ov_s), 1.0, 0.0).astype(BF16)
    p_hi = p_sum.astype(BF16)
    r1 = p_sum - p_hi.astype(F32)
    p_mid = r1.astype(BF16)
    p_lo = (r1 - p_mid.astype(F32)).astype(BF16)
    imp3 = _dot(jnp.concatenate([p_hi, p_mid, p_lo], axis=0), overlap)
    imp = imp3[0:tq] + imp3[tq:2 * tq] + imp3[2 * tq:3 * tq]
    s_idx = lax.broadcasted_iota(jnp.int32, (1, nsp), 1)
    cur = jnp.right_shift(t_tok, SLC_SHIFT)
    forced = (s_idx == 0) | (s_idx == cur) | (s_idx == cur - 1)
    imp = jnp.where(forced, FORCE_SCORE, imp)
    imp = jnp.where(s_idx > cur, -1.0, imp)

    span = WINDOW + tq
    w0 = pl.multiple_of(jnp.maximum(t0 - WINDOW, 0), tq)
    kw = kw_ref[pl.ds(w0, span), :]
    vw = vw_ref[pl.ds(w0, span), :]
    sw = _dot_nt(q, kw)
    dist = t_rows - (w0 + lax.broadcasted_iota(jnp.int32, (1, span), 1))
    _, o_win, _ = _masked_attend(sw, (dist >= 0) & (dist < WINDOW), vw)

    work = imp.T
    blk = lax.broadcasted_iota(jnp.int32, (nsp, tq), 0).astype(F32)
    unsel_t = jnp.ones((nsp, tq), F32)
    for _ in range(SLC_TOPK):
        mx = jnp.max(work, axis=0, keepdims=True)
        first = jnp.min(jnp.where(work == mx, blk, float(nsp)), axis=0, keepdims=True)
        chosen = blk == first
        unsel_t = jnp.where(chosen, 0.0, unsel_t)
        work = jnp.where(chosen, -3e38, work)
    unsel = unsel_t.T.astype(BF16)

    tk = SLC_KV_TILE
    q_aug = jnp.concatenate([jnp.concatenate([unsel] * hg, axis=0), q], axis=1)

    def slc_tile(j, carry, causal):
        k0 = pl.multiple_of(j * tk, tk)
        sc = _dot_nt(q_aug, ks_ref[pl.ds(k0, tk), :])
        if causal:
            sc = jnp.where(k0 + lax.broadcasted_iota(jnp.int32, (1, tk), 1) <= t_rows, sc, NEG_INF)
        m_i, acc = carry
        m_new = jnp.maximum(m_i, jnp.max(sc, axis=-1, keepdims=True))
        p = jnp.exp2(sc - m_new)
        return m_new, jnp.exp2(m_i - m_new) * acc + _dot(p.astype(BF16), vs_ref[pl.ds(k0, tk), :])

    init = (jnp.full((hg * tq, 1), NEG_INF, F32), jnp.zeros((hg * tq, V_AUG_W), F32))
    j_diag = t0 // tk
    carry = lax.fori_loop(0, j_diag, lambda j, c: slc_tile(j, c, False), init)
    _, acc_s = slc_tile(j_diag, carry, True)
    o_slc = acc_s[:, :HEAD_DIM] / acc_s[:, HEAD_DIM:HEAD_DIM + 1]

    gates = _sigmoid(gl_ref[...])
    outs = []
    for h in range(hg):
        r = slice(h * tq, (h + 1) * tq)
        outs.append(gates[:, 3 * h:3 * h + 1] * o_cmp[r] + gates[:, 3 * h + 1:3 * h + 2] * o_slc[r]
                    + gates[:, 3 * h + 2:3 * h + 3] * o_win[r])
    o_ref[...] = jnp.concatenate(outs, axis=1)


def _nsa_attn(qn, u_nsa, cmp_kv, ksa, vs, kwn, vw, batch, seq):
    m = u_nsa.shape[0]
    g = NSA_KV_HEADS
    nq = seq // Q_TILE
    ncp = cmp_kv.shape[3]
    gw = NSA_GROUP * HEAD_DIM
    gate_col0 = (NSA_W + 6 * KV_W) // LANES
    cmp_spec = lambda kv: pl.BlockSpec((None, None, None, ncp, V_AUG_W),
                                       lambda b, gi, i: (kv, gi, b, 0, 0))
    seq_spec = lambda w: pl.BlockSpec((None, seq, w), lambda b, gi, i: (gi, b, 0))
    return pl.pallas_call(
        functools.partial(_nsa_attn_kernel, seq=seq),
        grid=(batch, g, nq),
        in_specs=[pl.BlockSpec((None, NSA_GROUP, Q_TILE, HEAD_DIM),
                               lambda b, gi, i: (gi, 0, b * nq + i, 0)),
                  pl.BlockSpec((Q_TILE, LANES), lambda b, gi, i: (b * nq + i, gate_col0 + gi)),
                  cmp_spec(0), cmp_spec(1), seq_spec(KS_AUG_W), seq_spec(V_AUG_W),
                  seq_spec(HEAD_DIM), seq_spec(V_AUG_W)],
        out_specs=pl.BlockSpec((Q_TILE, gw), lambda b, gi, i: (b * nq + i, gi)),
        out_shape=jax.ShapeDtypeStruct((m, NSA_W), F32),
        compiler_params=_cparams("parallel", "parallel", "arbitrary"),
        name="nsa_attn",
    )(qn, u_nsa, cmp_kv, cmp_kv, ksa, vs, kwn, vw)


def _rwkv_prep_kernel(*refs, tiles_per_seq, has_vres):
    if has_vres:
        (z_ref, zp_ref, mu_ref, vec_ref, wup_ref, aup_ref, gup_ref, seg_ref, vf_ref, v0_ref, v1_ref,
         v2_ref, r_o, lw_o, k_o, v_o, kk_o, a_o, g_o, bonus_o) = refs
    else:
        (z_ref, zp_ref, mu_ref, vec_ref, wup_ref, aup_ref, gup_ref, seg_ref,
         r_o, lw_o, k_o, v_o, kk_o, a_o, g_o, bonus_o) = refs
    i = pl.program_id(0)
    z = z_ref[...]
    tm = z.shape[0]
    prev_last = jnp.where(i % tiles_per_seq == 0, 0.0, zp_ref[7:8, :])
    row = lax.broadcasted_iota(jnp.int32, (tm, 1), 0)
    z_prev = jnp.where(row == 0, prev_last, pltpu.roll(z, 1, axis=0))
    z = z + mu_ref[...] * (z_prev - z)
    w = RW_W
    r = z[:, 0:w]
    k = z[:, w:2 * w]
    v = z[:, 2 * w:3 * w]
    wd = z[:, 3 * w:3 * w + DECAY_RANK]
    ad = z[:, 3 * w + DECAY_RANK:3 * w + DECAY_RANK + ICLR_RANK]
    gd = z[:, 3 * w + DECAY_RANK + ICLR_RANK:]
    vec = vec_ref[...]
    w0, a0, k_k, k_a, r_k = (vec[n:n + 1] for n in range(5))
    if has_vres:
        lo = _dot(_dot(v.astype(BF16), v1_ref[...]).astype(BF16), v2_ref[...])
        v = v + (vf_ref[...] - v) * _sigmoid(v0_ref[...] + lo)
    wl = w0 + _dot(jnp.tanh(wd).astype(BF16), wup_ref[...])
    neg = -wl
    softplus = jnp.maximum(neg, 0.0) + jnp.log(1.0 + jnp.exp(-jnp.abs(neg)))
    lw_o[...] = -jnp.exp(-softplus - 0.5)
    a = _sigmoid(a0 + _dot(ad.astype(BF16), aup_ref[...]))
    g_o[...] = _dot(_sigmoid(gd).astype(BF16), gup_ref[...])
    seg = seg_ref[...]
    kk = k * k_k
    kh = k * (1.0 + (a - 1.0) * k_a)
    x = jnp.concatenate([kk * kk, r * kh * r_k], axis=0)
    x_hi = x.astype(BF16)
    x_r = x - x_hi.astype(F32)
    x_mid = x_r.astype(BF16)
    x_lo = (x_r - x_mid.astype(F32)).astype(BF16)
    sums = _dot(jnp.concatenate([x_hi, x_mid, x_lo], axis=0), seg)
    sums = sums[0:2 * tm] + sums[2 * tm:4 * tm] + sums[4 * tm:6 * tm]
    kk_o[...] = kk / jnp.maximum(jnp.sqrt(sums[0:tm]), 1e-12)
    bonus_o[...] = sums[tm:2 * tm] * v
    r_o[...] = r
    k_o[...] = kh
    v_o[...] = v
    a_o[...] = a


def _rwkv_prep(u_rw, mu, vec, wup, aup, gup, seg, vres, seq, tm=256):
    m = u_rw.shape[0]
    full = lambda i: (0, 0)
    row = lambda i: (i, 0)
    in_specs = [pl.BlockSpec((tm, RW_COLS), row),
                pl.BlockSpec((8, RW_COLS), lambda i: (jnp.maximum(i * (tm // 8) - 1, 0), 0)),
                pl.BlockSpec((1, RW_COLS), full), pl.BlockSpec((8, RW_W), full),
                pl.BlockSpec((DECAY_RANK, RW_W), full), pl.BlockSpec((ICLR_RANK, RW_W), full),
                pl.BlockSpec((GATE_RANK, RW_W), full), pl.BlockSpec((RW_W, RW_W), full)]
    args = [u_rw, u_rw, mu, vec, wup, aup, gup, seg]
    if vres is not None:
        v_first, v0, v1, v2 = vres
        in_specs += [pl.BlockSpec((tm, RW_W), row), pl.BlockSpec((1, RW_W), full),
                     pl.BlockSpec((RW_W, VRES_RANK), full), pl.BlockSpec((VRES_RANK, RW_W), full)]
        args += [v_first, v0, v1, v2]
    return pl.pallas_call(
        functools.partial(_rwkv_prep_kernel, tiles_per_seq=seq // tm, has_vres=vres is not None),
        grid=(m // tm,),
        in_specs=in_specs,
        out_specs=[pl.BlockSpec((tm, RW_W), row)] * 8,
        out_shape=[jax.ShapeDtypeStruct((m, RW_W), F32)] * 8,
        compiler_params=_cparams("parallel"),
        name="rwkv_prep",
    )(*args)


def _b(x):
    return x.astype(BF16)


def _tri_inverse(l_list, eye):
    c = eye.shape[0]
    ri = lax.broadcasted_iota(jnp.int32, (c, c), 0)
    ci = lax.broadcasted_iota(jnp.int32, (c, c), 1)
    same_blk = (ri // RW_SUB) == (ci // RW_SUB)
    ld = [jnp.where(same_blk, l, 0.0) for l in l_list]
    lo = [_b(l - d0) for l, d0 in zip(l_list, ld)]
    d = [eye + x for x in ld]
    p = [_b(x) for x in ld]
    sq = 1
    while 2 * sq < RW_SUB:
        p = [_b(_dot(x, x)) for x in p]
        d = [x + _dot(_b(x), y) for x, y in zip(d, p)]
        sq *= 2
    d = [_b(x) for x in d]
    n = [_dot(x, y) for x, y in zip(d, lo)]
    x = [eye + y for y in n]
    n = [_b(y) for y in n]
    terms = 2
    while terms < c // RW_SUB:
        n = [_b(_dot(y, y)) for y in n]
        x = [z + _dot(_b(z), y) for z, y in zip(x, n)]
        terms *= 2
    return [_dot(_b(z), y) for z, y in zip(x, d)]


def _rwkv_chunk_kernel(r_ref, lw_ref, k_ref, v_ref, kk_ref, a_ref, g_ref, bonus_ref, lnw_ref,
                       lnb_ref, o_ref, state_ref, *, chain_prec):
    t = pl.program_id(2)

    @pl.when(t == 0)
    def _():
        state_ref[...] = jnp.zeros_like(state_ref)

    rt_rows = r_ref.shape[0]
    c = RW_CHUNK
    hd = HEAD_DIM
    nh = LANES // hd
    ri = lax.broadcasted_iota(jnp.int32, (c, c), 0)
    ci = lax.broadcasted_iota(jnp.int32, (c, c), 1)
    tril_incl = ri >= ci
    tril_strict = ri > ci
    is_diag = ri == ci
    ones_incl = jnp.where(tril_incl, 1.0, 0.0).astype(BF16)
    eye = jnp.where(is_diag, 1.0, 0.0)
    if chain_prec is None:
        chain = lambda x: _b(x)
    else:
        chain = lambda x: x

    nc = rt_rows // c
    rows = [slice(cb * c, (cb + 1) * c) for cb in range(nc)]
    lanes = [slice(hh * hd, (hh + 1) * hd) for hh in range(nh)]
    items = [(cb, hh) for cb in range(nc) for hh in range(nh)]

    def cumulative(lw):
        lw_hi = _b(lw)
        lw_r = lw - lw_hi.astype(F32)
        lw_mid = _b(lw_r)
        lw_lo = _b(lw_r - lw_mid.astype(F32))
        g3 = _dot(ones_incl, jnp.concatenate([lw_hi, lw_mid, lw_lo], axis=1))
        return g3[:, :LANES] + g3[:, LANES:2 * LANES] + g3[:, 2 * LANES:]

    lw = [lw_ref[rs, :] for rs in rows]
    gcum = [cumulative(x) for x in lw]
    g_last = [x[c - 1:c] for x in gcum]
    kk = [kk_ref[rs, :] for rs in rows]
    k = [k_ref[rs, :] for rs in rows]
    v2 = [v_ref[rs, :] for rs in rows]
    b = [kk[cb] * a_ref[rows[cb], :] for cb in range(nc)]
    e_neg = [jnp.exp(-x) for x in gcum]
    e_end = [jnp.exp(g_last[cb] - gcum[cb]) for cb in range(nc)]
    a_t2 = [(-kk[cb]) * jnp.exp(gcum[cb] - lw[cb]) for cb in range(nc)]
    r_t2 = [r_ref[rows[cb], :] * jnp.exp(gcum[cb]) for cb in range(nc)]
    b_t2 = [b[cb] * e_neg[cb] for cb in range(nc)]
    k_t2 = [k[cb] * e_neg[cb] for cb in range(nc)]
    b_end2 = [b[cb] * e_end[cb] for cb in range(nc)]
    k_end2 = [k[cb] * e_end[cb] for cb in range(nc)]
    decay2 = [jnp.exp(x) for x in g_last]

    gram = [_dot_nt(_b(jnp.concatenate([a_t2[cb][:, lanes[hh]], r_t2[cb][:, lanes[hh]]], axis=0)),
                    _b(jnp.concatenate([b_t2[cb][:, lanes[hh]], k_t2[cb][:, lanes[hh]]], axis=0)))
            for cb, hh in items]
    l_ab = [jnp.where(tril_strict, x[:c, :c], 0.0) for x in gram]
    m_rb = [_b(jnp.where(tril_incl, x[c:, :c], 0.0)) for x in gram]
    lm = [_b(jnp.concatenate([jnp.where(tril_strict, x[:c, c:], 0.0),
                              jnp.where(tril_incl, x[c:, c:], 0.0)], axis=0)) for x in gram]
    v_b = [_b(v2[cb][:, lanes[hh]]) for cb, hh in items]
    lmv = [_dot(x, y) for x, y in zip(lm, v_b)]
    tinv = _tri_inverse(l_ab, eye)
    tw = [_dot(_b(tinv[i]), _b(jnp.concatenate([a_t2[cb][:, lanes[hh]], lmv[i][:c]], axis=1)))
          for i, (cb, hh) in enumerate(items)]
    tw_b = [_b(x) for x in tw]
    mb = [_dot(x, y) for x, y in zip(m_rb, tw_b)]
    q_eff = [chain(r_t2[cb][:, lanes[hh]] + mb[i][:, :hd]) for i, (cb, hh) in enumerate(items)]
    y_loc = [mb[i][:, hd:] + lmv[i][c:] for i in range(len(items))]
    a_c = [chain(jnp.where(is_diag, decay2[cb][:, lanes[hh]], 0.0)
                 + _dot_tn(tw_b[i][:, :hd], _b(b_end2[cb][:, lanes[hh]])))
           for i, (cb, hh) in enumerate(items)]
    b_c = [_dot_tn(jnp.concatenate([tw_b[i][:, hd:], v_b[i]], axis=0),
                   _b(jnp.concatenate([b_end2[cb][:, lanes[hh]], k_end2[cb][:, lanes[hh]]], axis=0)))
           for i, (cb, hh) in enumerate(items)]

    states = [state_ref[hh] for hh in range(nh)]
    for i, (cb, hh) in enumerate(items):
        s0 = chain(states[hh])
        y = _dot_nt(q_eff[i], s0, chain_prec) + y_loc[i]
        states[hh] = _dot(s0, a_c[i], chain_prec) + b_c[i]
        rs, ls = rows[cb], lanes[hh]
        mean = jnp.mean(y, axis=-1, keepdims=True)
        yc = y - mean
        var = jnp.mean(yc * yc, axis=-1, keepdims=True)
        yn = yc * lax.rsqrt(var + LNX_EPS) * lnw_ref[:, ls] + lnb_ref[:, ls]
        o_ref[rs, ls] = (yn + bonus_ref[rs, ls]) * g_ref[rs, ls]
    for hh in range(nh):
        state_ref[hh] = states[hh]


def _rwkv_chunk(r, lw, k, v, kk, a, g, bonus, lnw, lnb, batch, seq, chain_prec, rt=512):
    m = r.shape[0]
    nt = seq // rt
    pairs = RW_W // LANES
    row = pl.BlockSpec((rt, LANES), lambda b, p, t: (b * nt + t, p))
    vec = pl.BlockSpec((1, LANES), lambda b, p, t: (0, p))
    return pl.pallas_call(
        functools.partial(_rwkv_chunk_kernel, chain_prec=chain_prec),
        grid=(batch, pairs, nt),
        in_specs=[row] * 8 + [vec, vec],
        out_specs=row,
        out_shape=jax.ShapeDtypeStruct((m, RW_W), F32),
        scratch_shapes=[pltpu.VMEM((LANES // HEAD_DIM, HEAD_DIM, HEAD_DIM), F32)],
        compiler_params=_cparams("parallel", "parallel", "arbitrary"),
        name="rwkv_chunk",
    )(r, lw, k, v, kk, a, g, bonus, lnw, lnb)


def _merge_kernel(x_ref, on_ref, or_ref, mg_ref, pa_ref, pb_ref, wo_ref, o_ref):
    d = x_ref.shape[1]
    ya = _dot(on_ref[...].astype(BF16), pa_ref[...])
    yb = _dot(or_ref[...].astype(BF16), pb_ref[...])
    y = _sigmoid(mg_ref[:, :d]) * ya + _sigmoid(mg_ref[:, d:]) * yb
    o_ref[...] = x_ref[...] + _dot(y.astype(BF16), wo_ref[...])


def _merge(x, o_nsa, o_rw, u_mg, pa, pb, wo, tm=512):
    m, d = x.shape
    full = lambda i: (0, 0)
    row = lambda i: (i, 0)
    return pl.pallas_call(
        _merge_kernel,
        grid=(m // tm,),
        in_specs=[pl.BlockSpec((tm, d), row), pl.BlockSpec((tm, NSA_W), row),
                  pl.BlockSpec((tm, RW_W), row), pl.BlockSpec((tm, 2 * d), row),
                  pl.BlockSpec((NSA_W, d), full), pl.BlockSpec((RW_W, d), full),
                  pl.BlockSpec((d, d), full)],
        out_specs=pl.BlockSpec((tm, d), row),
        out_shape=jax.ShapeDtypeStruct((m, d), F32),
        compiler_params=_cparams("parallel"),
        name="merge_out",
    )(x, o_nsa, o_rw, u_mg, pa, pb, wo)


FFN_HALO = 16


def _ffn_kernel(x_ref, xp_ref, g_ref, wa_ref, wb_ref, ca_ref, cb_ref, wd_ref, o_ref, h_scr, *,
                tiles_per_seq):
    i = pl.program_id(0)
    j = pl.program_id(1)

    @pl.when(j == 0)
    def _():
        x = x_ref[...]
        h_scr[FFN_HALO:, :] = _rms(x, g_ref[...]).astype(BF16)
        hp = _rms(xp_ref[...], g_ref[...])
        h_scr[0:FFN_HALO, :] = jnp.where(i % tiles_per_seq == 0, 0.0, hp).astype(BF16)
        o_ref[...] = x

    h = h_scr[...]

    def conv(u, c_ref):
        cw = c_ref[...]
        return (cw[2:3] * u[FFN_HALO:] + cw[1:2] * pltpu.roll(u, 1, axis=0)[FFN_HALO:]
                + cw[0:1] * pltpu.roll(u, 2, axis=0)[FFN_HALO:])

    a = conv(_dot(h, wa_ref[...]), ca_ref)
    b = conv(_dot(h, wb_ref[...]), cb_ref)
    o_ref[...] += _dot((_silu(a) * b).astype(BF16), wd_ref[...])


def _ffn(x, gain, w_up, conv_w, w_down, seq, tm=512, tf=1408):
    m, d = x.shape
    nf = D_FF // tf
    return pl.pallas_call(
        functools.partial(_ffn_kernel, tiles_per_seq=seq // tm),
        grid=(m // tm, nf),
        in_specs=[pl.BlockSpec((tm, d), lambda i, j: (i, 0)),
                  pl.BlockSpec((FFN_HALO, d),
                               lambda i, j: (jnp.maximum(i * (tm // FFN_HALO) - 1, 0), 0)),
                  pl.BlockSpec((1, d), lambda i, j: (0, 0)),
                  pl.BlockSpec((d, tf), lambda i, j: (0, j)),
                  pl.BlockSpec((d, tf), lambda i, j: (0, j + nf)),
                  pl.BlockSpec((3, tf), lambda i, j: (0, j)),
                  pl.BlockSpec((3, tf), lambda i, j: (0, j + nf)),
                  pl.BlockSpec((tf, d), lambda i, j: (j, 0))],
        out_specs=pl.BlockSpec((tm, d), lambda i, j: (i, 0)),
        out_shape=jax.ShapeDtypeStruct((m, d), F32),
        scratch_shapes=[pltpu.VMEM((tm + FFN_HALO, d), BF16)],
        compiler_params=_cparams("parallel", "arbitrary"),
        name="conv_ffn",
    )(x, x, gain, w_up, w_up, conv_w, conv_w, w_down)


def _split_w_in(w):
    o = 0
    q = w[:, o:o + NSA_W]; o += NSA_W
    kv = w[:, o:o + 6 * KV_W]; o += 6 * KV_W
    n_gate = NSA_HEADS * 3
    gl = w[:, o:o + n_gate]; o += n_gate
    rw = w[:, o:o + RW_COLS]; o += RW_COLS
    mg = w[:, o:]
    per_g = n_gate // NSA_KV_HEADS
    pad = jnp.zeros((w.shape[0], LANES - per_g), w.dtype)
    gates = [jnp.concatenate([gl[:, g * per_g:(g + 1) * per_g], pad], axis=1)
             for g in range(NSA_KV_HEADS)]
    w_nsa = jnp.concatenate([q, kv] + gates, axis=1)
    return w_nsa.astype(BF16), rw.astype(BF16), mg.astype(BF16)


def kernel(x, norm_mix, norm_ffn, w_in, qk_gain, cmp_pe, cmp_w1, cmp_w2, rwkv_mu, rwkv_w0, rwkv_w_up,
           rwkv_a0, rwkv_a_up, rwkv_g_up, rwkv_k_k, rwkv_k_a, rwkv_r_k, rwkv_ln_w, rwkv_ln_b, vres_v0,
           vres_v1, vres_v2, proj_nsa, proj_rwkv, w_out, ffn_up, ffn_conv, ffn_down):
    batch, seq, d = x.shape
    depth = w_in.shape[0]
    m = batch * seq
    assert seq % SLC_KV_TILE == 0 and seq // SLC_BLOCK <= LANES and seq >= WINDOW + Q_TILE
    xf = x.reshape(m, d)
    half = CMP_STRIDE * HEAD_DIM
    hi = lax.broadcasted_iota(jnp.int32, (RW_W, RW_W), 0) // HEAD_DIM
    hj = lax.broadcasted_iota(jnp.int32, (RW_W, RW_W), 1) // HEAD_DIM
    seg = (hi == hj).astype(BF16)
    v_first = None
    for l in range(depth):
        w_nsa, w_rw, w_mg = _split_w_in(w_in[l])
        u_nsa, u_rw, u_mg = _inproj(xf, norm_mix[l][None], w_nsa, w_rw, w_mg)

        qn, cmp_in, ksa, vs, kwn, vw = _nsa_prep(u_nsa, qk_gain[l], seq)
        z = cmp_in.reshape(2, NSA_KV_HEADS, batch, seq // CMP_STRIDE, half)
        w1 = cmp_w1[l]
        w1ab = jnp.concatenate([w1[:, :CMP_STRIDE].reshape(2, half, HEAD_DIM),
                                w1[:, CMP_STRIDE:].reshape(2, half, HEAD_DIM)], axis=2).astype(BF16)
        w1flat = w1.reshape(2, 2 * half, HEAD_DIM).astype(BF16)
        pe8 = jnp.broadcast_to(cmp_pe[l].reshape(2, 1, 2 * half), (2, 8, 2 * half))
        cmp_kv = _nsa_compress(z, w1ab, pe8, w1flat, cmp_w2[l].astype(BF16), qk_gain[l])
        o_nsa = _nsa_attn(qn, u_nsa, cmp_kv, ksa, vs, kwn, vw, batch, seq)

        zero = jnp.zeros((RW_W,), F32)
        vec = jnp.stack([rwkv_w0[l], rwkv_a0[l], rwkv_k_k[l], rwkv_k_a[l], rwkv_r_k[l].reshape(RW_W),
                         zero, zero, zero])
        vres = None
        if l > 0:
            vres = (v_first, vres_v0[l - 1][None], vres_v1[l - 1].astype(BF16),
                    vres_v2[l - 1].astype(BF16))
        r, lw, kh, v, kk, a, g, bonus = _rwkv_prep(
            u_rw, rwkv_mu[l][None], vec, rwkv_w_up[l].astype(BF16), rwkv_a_up[l].astype(BF16),
            rwkv_g_up[l].astype(BF16), seg, vres, seq)
        if l == 0:
            v_first = v
        o_rw = _rwkv_chunk(r, lw, kh, v, kk, a, g, bonus, rwkv_ln_w[l][None], rwkv_ln_b[l][None],
                           batch, seq, None)

        xf = _merge(xf, o_nsa, o_rw, u_mg, proj_nsa[l].astype(BF16), proj_rwkv[l].astype(BF16),
                    w_out[l].astype(BF16))
        xf = _ffn(xf, norm_ffn[l][None], ffn_up[l].astype(BF16), ffn_conv[l],
                  ffn_down[l].astype(BF16), seq)
    return xf.reshape(batch, seq, d)
```

```python
import functools

import jax
import jax.numpy as jnp
from jax import lax
from jax.experimental import pallas as pl
from jax.experimental.pallas import tpu as pltpu

F32 = jnp.float32
BF16 = jnp.bfloat16
HIGHEST = lax.Precision.HIGHEST

LANES = 128
VMEM_LIMIT_BYTES = 56 * 1024 * 1024

NSA_HEADS = 8
NSA_KV_HEADS = 2
NSA_GROUP = NSA_HEADS // NSA_KV_HEADS
HEAD_DIM = 64
CMP_BLOCK = 32
CMP_STRIDE = 16
SLC_BLOCK = 64
SLC_SHIFT = 6
SLC_TOPK = 16
WINDOW = 512
FORCE_SCORE = 1e4
NEG_INF = -1e30
LOG2_E = 1.4426950408889634
RWKV_HEADS = 8
DECAY_RANK = 64
ICLR_RANK = 64
GATE_RANK = 128
VRES_RANK = 32
LNX_EPS = 1e-5 * HEAD_DIM
D_FF = 2816
EPS = 1e-6

NSA_W = NSA_HEADS * HEAD_DIM
KV_W = NSA_KV_HEADS * HEAD_DIM
RW_W = RWKV_HEADS * HEAD_DIM
RW_COLS = 3 * RW_W + DECAY_RANK + ICLR_RANK + GATE_RANK
NSA_SLAB = NSA_W + 6 * KV_W + NSA_KV_HEADS * LANES

Q_TILE = 256
SLC_KV_TILE = 1024
RW_CHUNK = 64
RW_SUB = 16


def _cparams(*sem):
    return pltpu.CompilerParams(dimension_semantics=sem, vmem_limit_bytes=VMEM_LIMIT_BYTES)


def _dot(a, b, precision=None):
    return lax.dot_general(a, b, (((1,), (0,)), ((), ())), precision=precision,
                           preferred_element_type=F32)


def _dot_nt(a, b, precision=None):
    return lax.dot_general(a, b, (((1,), (1,)), ((), ())), precision=precision,
                           preferred_element_type=F32)


def _dot_tn(a, b, precision=None):
    return lax.dot_general(a, b, (((0,), (0,)), ((), ())), precision=precision,
                           preferred_element_type=F32)


def _rms(x, gain):
    return x * lax.rsqrt(jnp.mean(x * x, axis=-1, keepdims=True) + EPS) * gain


def _sigmoid(x):
    return 1.0 / (1.0 + jnp.exp(-x))


def _silu(x):
    return x * _sigmoid(x)


def _inproj_kernel(x_ref, g_ref, wn_ref, wr_ref, wm_ref, on_ref, or_ref, om_ref):
    h = _rms(x_ref[...], g_ref[...]).astype(BF16)
    on_ref[...] = _dot(h, wn_ref[...])
    or_ref[...] = _dot(h, wr_ref[...])
    om_ref[...] = _dot(h, wm_ref[...])


def _inproj(x, gain, w_nsa, w_rw, w_mg, tm=256):
    m, d = x.shape
    outs = [w_nsa.shape[1], w_rw.shape[1], w_mg.shape[1]]
    full = lambda i: (0, 0)
    return pl.pallas_call(
        _inproj_kernel,
        grid=(m // tm,),
        in_specs=[pl.BlockSpec((tm, d), lambda i: (i, 0)), pl.BlockSpec((1, d), full)]
        + [pl.BlockSpec((d, n), full) for n in outs],
        out_specs=[pl.BlockSpec((tm, n), lambda i: (i, 0)) for n in outs],
        out_shape=[jax.ShapeDtypeStruct((m, n), F32) for n in outs],
        compiler_params=_cparams("parallel"),
        name="inproj",
    )(x, gain, w_nsa, w_rw, w_mg)


MASK_BIG = 2.0 ** 99
KS_AUG_W = LANES + HEAD_DIM


V_AUG_W = 2 * HEAD_DIM


def _nsa_prep_kernel(q_ref, kc_ref, vc_ref, ks_ref, vs_ref, kw_ref, vw_ref, gn_ref, qn_ref, cmp_ref,
                     ksa_ref, vs_o_ref, kwn_ref, vw_o_ref, *, tiles_per_seq):
    gn = gn_ref[...]
    tm = ks_ref.shape[0]
    pos = (pl.program_id(0) % tiles_per_seq) * tm + lax.broadcasted_iota(jnp.int32, (tm, LANES), 0)
    blk_col = lax.broadcasted_iota(jnp.int32, (tm, LANES), 1)
    onehot = jnp.where(jnp.right_shift(pos, SLC_SHIFT) == blk_col, -MASK_BIG, 0.0).astype(BF16)
    ones = jnp.ones((tm, HEAD_DIM), BF16)
    q_scale = HEAD_DIM ** -0.5 * LOG2_E
    for h in range(NSA_HEADS):
        qh = q_ref[:, h * HEAD_DIM:(h + 1) * HEAD_DIM]
        qn_ref[h // NSA_GROUP, h % NSA_GROUP] = (_rms(qh, gn[0:1]) * q_scale).astype(BF16)
    for g in range(NSA_KV_HEADS):
        sl = slice(g * HEAD_DIM, (g + 1) * HEAD_DIM)
        cmp_ref[0, g] = kc_ref[:, sl]
        cmp_ref[1, g] = vc_ref[:, sl]
        ksa_ref[g, :, :LANES] = onehot
        ksa_ref[g, :, LANES:] = _rms(ks_ref[:, sl], gn[2:3]).astype(BF16)
        kwn_ref[g] = _rms(kw_ref[:, sl], gn[3:4]).astype(BF16)
        vs_o_ref[g, :, :HEAD_DIM] = vs_ref[:, sl].astype(BF16)
        vs_o_ref[g, :, HEAD_DIM:] = ones
        vw_o_ref[g, :, :HEAD_DIM] = vw_ref[:, sl].astype(BF16)
        vw_o_ref[g, :, HEAD_DIM:] = ones


def _nsa_prep(u_nsa, qk_gain, seq, tm=512):
    m = u_nsa.shape[0]
    g = NSA_KV_HEADS
    col0 = NSA_W // KV_W
    in_specs = [pl.BlockSpec((tm, NSA_W), lambda i: (i, 0))]
    in_specs += [pl.BlockSpec((tm, KV_W), functools.partial(lambda i, c: (i, c), c=col0 + c))
                 for c in range(6)]
    in_specs.append(pl.BlockSpec((4, HEAD_DIM), lambda i: (0, 0)))
    spec3 = lambda w: pl.BlockSpec((g, tm, w), lambda i: (0, i, 0))
    shape3 = lambda w: jax.ShapeDtypeStruct((g, m, w), BF16)
    return pl.pallas_call(
        functools.partial(_nsa_prep_kernel, tiles_per_seq=seq // tm),
        grid=(m // tm,),
        in_specs=in_specs,
        out_specs=[pl.BlockSpec((g, NSA_GROUP, tm, HEAD_DIM), lambda i: (0, 0, i, 0)),
                   pl.BlockSpec((2, g, tm, HEAD_DIM), lambda i: (0, 0, i, 0)),
                   spec3(KS_AUG_W), spec3(V_AUG_W), spec3(HEAD_DIM), spec3(V_AUG_W)],
        out_shape=[jax.ShapeDtypeStruct((g, NSA_GROUP, m, HEAD_DIM), BF16),
                   jax.ShapeDtypeStruct((2, g, m, HEAD_DIM), F32),
                   shape3(KS_AUG_W), shape3(V_AUG_W), shape3(HEAD_DIM), shape3(V_AUG_W)],
        compiler_params=_cparams("parallel"),
        name="nsa_prep",
    )(*([u_nsa] * 7), qk_gain)


def _nsa_compress_kernel(z_ref, w1ab_ref, pe_ref, w1_ref, w2_ref, gn_ref, o_ref):
    kv = pl.program_id(0)
    nhalf = z_ref.shape[0]
    p = _dot(z_ref[...].astype(BF16), w1ab_ref[...])
    a = p[:, :HEAD_DIM]
    b_next = pltpu.roll(p[:, HEAD_DIM:], nhalf - 1, axis=0)
    c = _dot(pe_ref[...].astype(BF16), w1_ref[...])[0:1]
    hid = _silu(a + b_next + c)
    o = _dot(hid.astype(BF16), w2_ref[...])
    o_ref[:, :HEAD_DIM] = jnp.where(kv == 0, _rms(o, gn_ref[1:2]), o).astype(BF16)
    o_ref[:, HEAD_DIM:] = jnp.ones((nhalf, HEAD_DIM), BF16)


def _nsa_compress(z, w1ab, pe8, w1flat, w2, qk_gain):
    _, g, b, nhalf, zw = z.shape
    sq = lambda *shape: pl.BlockSpec((None,) + shape, lambda kv, gi, bi: (kv,) + (0,) * len(shape))
    return pl.pallas_call(
        _nsa_compress_kernel,
        grid=(2, g, b),
        in_specs=[pl.BlockSpec((None, None, None, nhalf, zw), lambda kv, gi, bi: (kv, gi, bi, 0, 0)),
                  sq(zw, 2 * HEAD_DIM), sq(8, 2 * zw), sq(2 * zw, HEAD_DIM), sq(HEAD_DIM, HEAD_DIM),
                  pl.BlockSpec((4, HEAD_DIM), lambda kv, gi, bi: (0, 0))],
        out_specs=pl.BlockSpec((None, None, None, nhalf, V_AUG_W),
                               lambda kv, gi, bi: (kv, gi, bi, 0, 0)),
        out_shape=jax.ShapeDtypeStruct((2, g, b, nhalf, V_AUG_W), BF16),
        compiler_params=_cparams("parallel", "parallel", "parallel"),
        name="nsa_compress",
    )(z, w1ab, pe8, w1flat, w2, qk_gain)


def _masked_attend(s, mask, v_aug):
    s = jnp.where(mask, s, NEG_INF)
    mx = jnp.max(s, axis=-1, keepdims=True)
    e = jnp.exp2(s - mx)
    ov = _dot(e.astype(BF16), v_aug)
    inv = jnp.where(mx > 0.5 * NEG_INF, 1.0 / ov[:, HEAD_DIM:HEAD_DIM + 1], 0.0)
    return e, ov[:, :HEAD_DIM] * inv, inv


def _nsa_attn_kernel(q_ref, gl_ref, kc_ref, vc_ref, ks_ref, vs_ref, kw_ref, vw_ref, o_ref, *, seq):
    i = pl.program_id(2)
    hg, tq = q_ref.shape[0], q_ref.shape[1]
    t0 = i * tq
    q = q_ref[...].reshape(hg * tq, HEAD_DIM)
    t_tok = t0 + lax.broadcasted_iota(jnp.int32, (tq, 1), 0)
    t_rows = jnp.concatenate([t_tok] * hg, axis=0)

    ncp = kc_ref.shape[0]
    s = _dot_nt(q, kc_ref[:, :HEAD_DIM])
    n_idx = lax.broadcasted_iota(jnp.int32, (1, ncp), 1)
    cmp_valid = (n_idx * CMP_STRIDE + (CMP_BLOCK - 1)) <= t_rows
    e_cmp, o_cmp, inv_cmp = _masked_attend(s, cmp_valid, vc_ref[...])

    nsp = LANES
    p_sum = e_cmp[0:tq] * inv_cmp[0:tq]
    for h in range(1, hg):
        p_sum = p_sum + e_cmp[h * tq:(h + 1) * tq] * inv_cmp[h * tq:(h + 1) * tq]
    ov_n = lax.broadcasted_iota(jnp.int32, (ncp, nsp), 0) * CMP_STRIDE
    ov_s = lax.broadcasted_iota(jnp.int32, (ncp, nsp), 1) * SLC_BLOCK
    overlap = jnp.where((ov_n < ov_s + SLC_BLOCK) & (ov_n + CMP_BLOCK > ov_s), 1.0, 0.0).astype(BF16)
    p_hi = p_sum.astype(BF16)
    r1 = p_sum - p_hi.astype(F32)
    p_mid = r1.astype(BF16)
    p_lo = (r1 - p_mid.astype(F32)).astype(BF16)
    imp3 = _dot(jnp.concatenate([p_hi, p_mid, p_lo], axis=0), overlap)
    imp = imp3[0:tq] + imp3[tq:2 * tq] + imp3[2 * tq:3 * tq]
    s_idx = lax.broadcasted_iota(jnp.int32, (1, nsp), 1)
    cur = jnp.right_shift(t_tok, SLC_SHIFT)
    forced = (s_idx == 0) | (s_idx == cur) | (s_idx == cur - 1)
    imp = jnp.where(forced, FORCE_SCORE, imp)
    imp = jnp.where(s_idx > cur, -1.0, imp)

    work = imp.T
    blk = lax.broadcasted_iota(jnp.int32, (nsp, tq), 0).astype(F32)
    unsel_t = jnp.ones((nsp, tq), F32)
    for _ in range(SLC_TOPK):
        mx = jnp.max(work, axis=0, keepdims=True)
        first = jnp.min(jnp.where(work == mx, blk, float(nsp)), axis=0, keepdims=True)
        chosen = blk == first
        unsel_t = jnp.where(chosen, 0.0, unsel_t)
        work = jnp.where(chosen, -3e38, work)
    unsel = unsel_t.T.astype(BF16)

    span = WINDOW + tq
    w0 = pl.multiple_of(jnp.maximum(t0 - WINDOW, 0), tq)
    kw = kw_ref[pl.ds(w0, span), :]
    vw = vw_ref[pl.ds(w0, span), :]
    sw = _dot_nt(q, kw)
    dist = t_rows - (w0 + lax.broadcasted_iota(jnp.int32, (1, span), 1))
    _, o_win, _ = _masked_attend(sw, (dist >= 0) & (dist < WINDOW), vw)

    tk = SLC_KV_TILE
    q_aug = jnp.concatenate([jnp.concatenate([unsel] * hg, axis=0), q], axis=1)

    def slc_tile(j, carry, causal):
        k0 = pl.multiple_of(j * tk, tk)
        sc = _dot_nt(q_aug, ks_ref[pl.ds(k0, tk), :])
        if causal:
            sc = jnp.where(k0 + lax.broadcasted_iota(jnp.int32, (1, tk), 1) <= t_rows, sc, NEG_INF)
        m_i, acc = carry
        m_new = jnp.maximum(m_i, jnp.max(sc, axis=-1, keepdims=True))
        p = jnp.exp2(sc - m_new)
        return m_new, jnp.exp2(m_i - m_new) * acc + _dot(p.astype(BF16), vs_ref[pl.ds(k0, tk), :])

    init = (jnp.full((hg * tq, 1), NEG_INF, F32), jnp.zeros((hg * tq, V_AUG_W), F32))
    j_diag = t0 // tk
    carry = lax.fori_loop(0, j_diag, lambda j, c: slc_tile(j, c, False), init)
    _, acc_s = slc_tile(j_diag, carry, True)
    o_slc = acc_s[:, :HEAD_DIM] / acc_s[:, HEAD_DIM:HEAD_DIM + 1]

    gates = _sigmoid(gl_ref[...])
    outs = []
    for h in range(hg):
        r = slice(h * tq, (h + 1) * tq)
        outs.append(gates[:, 3 * h:3 * h + 1] * o_cmp[r] + gates[:, 3 * h + 1:3 * h + 2] * o_slc[r]
                    + gates[:, 3 * h + 2:3 * h + 3] * o_win[r])
    o_ref[...] = jnp.concatenate(outs, axis=1)


def _nsa_attn(qn, u_nsa, cmp_kv, ksa, vs, kwn, vw, batch, seq):
    m = u_nsa.shape[0]
    g = NSA_KV_HEADS
    nq = seq // Q_TILE
    ncp = cmp_kv.shape[3]
    gw = NSA_GROUP * HEAD_DIM
    gate_col0 = (NSA_W + 6 * KV_W) // LANES
    cmp_spec = lambda kv: pl.BlockSpec((None, None, None, ncp, V_AUG_W),
                                       lambda b, gi, i: (kv, gi, b, 0, 0))
    seq_spec = lambda w: pl.BlockSpec((None, seq, w), lambda b, gi, i: (gi, b, 0))
    return pl.pallas_call(
        functools.partial(_nsa_attn_kernel, seq=seq),
        grid=(batch, g, nq),
        in_specs=[pl.BlockSpec((None, NSA_GROUP, Q_TILE, HEAD_DIM),
                               lambda b, gi, i: (gi, 0, b * nq + i, 0)),
                  pl.BlockSpec((Q_TILE, LANES), lambda b, gi, i: (b * nq + i, gate_col0 + gi)),
                  cmp_spec(0), cmp_spec(1), seq_spec(KS_AUG_W), seq_spec(V_AUG_W),
                  seq_spec(HEAD_DIM), seq_spec(V_AUG_W)],
        out_specs=pl.BlockSpec((Q_TILE, gw), lambda b, gi, i: (b * nq + i, gi)),
        out_shape=jax.ShapeDtypeStruct((m, NSA_W), F32),
        compiler_params=_cparams("parallel", "parallel", "arbitrary"),
        name="nsa_attn",
    )(qn, u_nsa, cmp_kv, cmp_kv, ksa, vs, kwn, vw)


def _rwkv_prep_kernel(*refs, tiles_per_seq, has_vres):
    if has_vres:
        (z_ref, zp_ref, mu_ref, vec_ref, wup_ref, aup_ref, gup_ref, seg_ref, vf_ref, v0_ref, v1_ref,
         v2_ref, r_o, lw_o, k_o, v_o, kk_o, a_o, g_o, bonus_o) = refs
    else:
        (z_ref, zp_ref, mu_ref, vec_ref, wup_ref, aup_ref, gup_ref, seg_ref,
         r_o, lw_o, k_o, v_o, kk_o, a_o, g_o, bonus_o) = refs
    i = pl.program_id(0)
    z = z_ref[...]
    tm = z.shape[0]
    prev_last = jnp.where(i % tiles_per_seq == 0, 0.0, zp_ref[7:8, :])
    row = lax.broadcasted_iota(jnp.int32, (tm, 1), 0)
    z_prev = jnp.where(row == 0, prev_last, pltpu.roll(z, 1, axis=0))
    z = z + mu_ref[...] * (z_prev - z)
    w = RW_W
    r = z[:, 0:w]
    k = z[:, w:2 * w]
    v = z[:, 2 * w:3 * w]
    wd = z[:, 3 * w:3 * w + DECAY_RANK]
    ad = z[:, 3 * w + DECAY_RANK:3 * w + DECAY_RANK + ICLR_RANK]
    gd = z[:, 3 * w + DECAY_RANK + ICLR_RANK:]
    vec = vec_ref[...]
    w0, a0, k_k, k_a, r_k = (vec[n:n + 1] for n in range(5))
    if has_vres:
        lo = _dot(_dot(v.astype(BF16), v1_ref[...]).astype(BF16), v2_ref[...])
        v = v + (vf_ref[...] - v) * _sigmoid(v0_ref[...] + lo)
    wl = w0 + _dot(jnp.tanh(wd).astype(BF16), wup_ref[...])
    neg = -wl
    softplus = jnp.maximum(neg, 0.0) + jnp.log(1.0 + jnp.exp(-jnp.abs(neg)))
    lw_o[...] = -jnp.exp(-softplus - 0.5)
    a = _sigmoid(a0 + _dot(ad.astype(BF16), aup_ref[...]))
    g_o[...] = _dot(_sigmoid(gd).astype(BF16), gup_ref[...])
    seg = seg_ref[...]
    kk = k * k_k
    kh = k * (1.0 + (a - 1.0) * k_a)
    x = jnp.concatenate([kk * kk, r * kh * r_k], axis=0)
    x_hi = x.astype(BF16)
    x_r = x - x_hi.astype(F32)
    x_mid = x_r.astype(BF16)
    x_lo = (x_r - x_mid.astype(F32)).astype(BF16)
    sums = _dot(jnp.concatenate([x_hi, x_mid, x_lo], axis=0), seg)
    sums = sums[0:2 * tm] + sums[2 * tm:4 * tm] + sums[4 * tm:6 * tm]
    kk_o[...] = kk / jnp.maximum(jnp.sqrt(sums[0:tm]), 1e-12)
    bonus_o[...] = sums[tm:2 * tm] * v
    r_o[...] = r
    k_o[...] = kh
    v_o[...] = v
    a_o[...] = a


def _rwkv_prep(u_rw, mu, vec, wup, aup, gup, seg, vres, seq, tm=256):
    m = u_rw.shape[0]
    full = lambda i: (0, 0)
    row = lambda i: (i, 0)
    in_specs = [pl.BlockSpec((tm, RW_COLS), row),
                pl.BlockSpec((8, RW_COLS), lambda i: (jnp.maximum(i * (tm // 8) - 1, 0), 0)),
                pl.BlockSpec((1, RW_COLS), full), pl.BlockSpec((8, RW_W), full),
                pl.BlockSpec((DECAY_RANK, RW_W), full), pl.BlockSpec((ICLR_RANK, RW_W), full),
                pl.BlockSpec((GATE_RANK, RW_W), full), pl.BlockSpec((RW_W, RW_W), full)]
    args = [u_rw, u_rw, mu, vec, wup, aup, gup, seg]
    if vres is not None:
        v_first, v0, v1, v2 = vres
        in_specs += [pl.BlockSpec((tm, RW_W), row), pl.BlockSpec((1, RW_W), full),
                     pl.BlockSpec((RW_W, VRES_RANK), full), pl.BlockSpec((VRES_RANK, RW_W), full)]
        args += [v_first, v0, v1, v2]
    return pl.pallas_call(
        functools.partial(_rwkv_prep_kernel, tiles_per_seq=seq // tm, has_vres=vres is not None),
        grid=(m // tm,),
        in_specs=in_specs,
        out_specs=[pl.BlockSpec((tm, RW_W), row)] * 8,
        out_shape=[jax.ShapeDtypeStruct((m, RW_W), F32)] * 8,
        compiler_params=_cparams("parallel"),
        name="rwkv_prep",
    )(*args)


def _b(x):
    return x.astype(BF16)


def _tri_inverse(l_list, eye):
    c = eye.shape[0]
    ri = lax.broadcasted_iota(jnp.int32, (c, c), 0)
    ci = lax.broadcasted_iota(jnp.int32, (c, c), 1)
    same_blk = (ri // RW_SUB) == (ci // RW_SUB)
    ld = [jnp.where(same_blk, l, 0.0) for l in l_list]
    lo = [_b(l - d0) for l, d0 in zip(l_list, ld)]
    d = [eye + x for x in ld]
    p = [_b(x) for x in ld]
    sq = 1
    while 2 * sq < RW_SUB:
        p = [_b(_dot(x, x)) for x in p]
        d = [x + _dot(_b(x), y) for x, y in zip(d, p)]
        sq *= 2
    d = [_b(x) for x in d]
    n = [_dot(x, y) for x, y in zip(d, lo)]
    x = [eye + y for y in n]
    n = [_b(y) for y in n]
    terms = 2
    while terms < c // RW_SUB:
        n = [_b(_dot(y, y)) for y in n]
        x = [z + _dot(_b(z), y) for z, y in zip(x, n)]
        terms *= 2
    return [_dot(_b(z), y) for z, y in zip(x, d)]


def _rwkv_chunk_kernel(r_ref, lw_ref, k_ref, v_ref, kk_ref, a_ref, g_ref, bonus_ref, lnw_ref,
                       lnb_ref, o_ref, state_ref, *, chain_prec):
    t = pl.program_id(2)

    @pl.when(t == 0)
    def _():
        state_ref[...] = jnp.zeros_like(state_ref)

    rt_rows = r_ref.shape[0]
    c = RW_CHUNK
    hd = HEAD_DIM
    nh = LANES // hd
    ri = lax.broadcasted_iota(jnp.int32, (c, c), 0)
    ci = lax.broadcasted_iota(jnp.int32, (c, c), 1)
    tril_incl = ri >= ci
    tril_strict = ri > ci
    is_diag = ri == ci
    ones_incl = jnp.where(tril_incl, 1.0, 0.0).astype(BF16)
    eye = jnp.where(is_diag, 1.0, 0.0)
    if chain_prec is None:
        chain = lambda x: _b(x)
    else:
        chain = lambda x: x

    nc = rt_rows // c
    rows = [slice(cb * c, (cb + 1) * c) for cb in range(nc)]
    lanes = [slice(hh * hd, (hh + 1) * hd) for hh in range(nh)]
    items = [(cb, hh) for cb in range(nc) for hh in range(nh)]

    def cumulative(lw):
        lw_hi = _b(lw)
        lw_r = lw - lw_hi.astype(F32)
        lw_mid = _b(lw_r)
        lw_lo = _b(lw_r - lw_mid.astype(F32))
        g3 = _dot(ones_incl, jnp.concatenate([lw_hi, lw_mid, lw_lo], axis=1))
        return g3[:, :LANES] + g3[:, LANES:2 * LANES] + g3[:, 2 * LANES:]

    lw = [lw_ref[rs, :] for rs in rows]
    gcum = [cumulative(x) for x in lw]
    g_last = [x[c - 1:c] for x in gcum]
    kk = [kk_ref[rs, :] for rs in rows]
    k = [k_ref[rs, :] for rs in rows]
    v2 = [v_ref[rs, :] for rs in rows]
    b = [kk[cb] * a_ref[rows[cb], :] for cb in range(nc)]
    e_neg = [jnp.exp(-x) for x in gcum]
    e_end = [jnp.exp(g_last[cb] - gcum[cb]) for cb in range(nc)]
    a_t2 = [(-kk[cb]) * jnp.exp(gcum[cb] - lw[cb]) for cb in range(nc)]
    r_t2 = [r_ref[rows[cb], :] * jnp.exp(gcum[cb]) for cb in range(nc)]
    b_t2 = [b[cb] * e_neg[cb] for cb in range(nc)]
    k_t2 = [k[cb] * e_neg[cb] for cb in range(nc)]
    b_end2 = [b[cb] * e_end[cb] for cb in range(nc)]
    k_end2 = [k[cb] * e_end[cb] for cb in range(nc)]
    decay2 = [jnp.exp(x) for x in g_last]

    gram = [_dot_nt(_b(jnp.concatenate([a_t2[cb][:, lanes[hh]], r_t2[cb][:, lanes[hh]]], axis=0)),
                    _b(jnp.concatenate([b_t2[cb][:, lanes[hh]], k_t2[cb][:, lanes[hh]]], axis=0)))
            for cb, hh in items]
    l_ab = [jnp.where(tril_strict, x[:c, :c], 0.0) for x in gram]
    m_rb = [_b(jnp.where(tril_incl, x[c:, :c], 0.0)) for x in gram]
    lm = [_b(jnp.concatenate([jnp.where(tril_strict, x[:c, c:], 0.0),
                              jnp.where(tril_incl, x[c:, c:], 0.0)], axis=0)) for x in gram]
    v_b = [_b(v2[cb][:, lanes[hh]]) for cb, hh in items]
    lmv = [_dot(x, y) for x, y in zip(lm, v_b)]
    tinv = _tri_inverse(l_ab, eye)
    tw = [_dot(_b(tinv[i]), _b(jnp.concatenate([a_t2[cb][:, lanes[hh]], lmv[i][:c]], axis=1)))
          for i, (cb, hh) in enumerate(items)]
    tw_b = [_b(x) for x in tw]
    mb = [_dot(x, y) for x, y in zip(m_rb, tw_b)]
    q_eff = [chain(r_t2[cb][:, lanes[hh]] + mb[i][:, :hd]) for i, (cb, hh) in enumerate(items)]
    y_loc = [mb[i][:, hd:] + lmv[i][c:] for i in range(len(items))]
    a_c = [chain(jnp.where(is_diag, decay2[cb][:, lanes[hh]], 0.0)
                 + _dot_tn(tw_b[i][:, :hd], _b(b_end2[cb][:, lanes[hh]])))
           for i, (cb, hh) in enumerate(items)]
    b_c = [_dot_tn(jnp.concatenate([tw_b[i][:, hd:], v_b[i]], axis=0),
                   _b(jnp.concatenate([b_end2[cb][:, lanes[hh]], k_end2[cb][:, lanes[hh]]], axis=0)))
           for i, (cb, hh) in enumerate(items)]

    states = [state_ref[hh] for hh in range(nh)]
    for i, (cb, hh) in enumerate(items):
        s0 = chain(states[hh])
        y = _dot_nt(q_eff[i], s0, chain_prec) + y_loc[i]
        states[hh] = _dot(s0, a_c[i], chain_prec) + b_c[i]
        rs, ls = rows[cb], lanes[hh]
        mean = jnp.mean(y, axis=-1, keepdims=True)
        yc = y - mean
        var = jnp.mean(yc * yc, axis=-1, keepdims=True)
        yn = yc * lax.rsqrt(var + LNX_EPS) * lnw_ref[:, ls] + lnb_ref[:, ls]
        o_ref[rs, ls] = (yn + bonus_ref[rs, ls]) * g_ref[rs, ls]
    for hh in range(nh):
        state_ref[hh] = states[hh]


def _rwkv_chunk(r, lw, k, v, kk, a, g, bonus, lnw, lnb, batch, seq, chain_prec, rt=1024):
    m = r.shape[0]
    nt = seq // rt
    pairs = RW_W // LANES
    row = pl.BlockSpec((rt, LANES), lambda b, p, t: (b * nt + t, p))
    vec = pl.BlockSpec((1, LANES), lambda b, p, t: (0, p))
    return pl.pallas_call(
        functools.partial(_rwkv_chunk_kernel, chain_prec=chain_prec),
        grid=(batch, pairs, nt),
        in_specs=[row] * 8 + [vec, vec],
        out_specs=row,
        out_shape=jax.ShapeDtypeStruct((m, RW_W), F32),
        scratch_shapes=[pltpu.VMEM((LANES // HEAD_DIM, HEAD_DIM, HEAD_DIM), F32)],
        compiler_params=_cparams("parallel", "parallel", "arbitrary"),
        name="rwkv_chunk",
    )(r, lw, k, v, kk, a, g, bonus, lnw, lnb)


def _merge_kernel(x_ref, on_ref, or_ref, mg_ref, pa_ref, pb_ref, wo_ref, o_ref):
    d = x_ref.shape[1]
    ya = _dot(on_ref[...].astype(BF16), pa_ref[...])
    yb = _dot(or_ref[...].astype(BF16), pb_ref[...])
    y = _sigmoid(mg_ref[:, :d]) * ya + _sigmoid(mg_ref[:, d:]) * yb
    o_ref[...] = x_ref[...] + _dot(y.astype(BF16), wo_ref[...])


def _merge(x, o_nsa, o_rw, u_mg, pa, pb, wo, tm=512):
    m, d = x.shape
    full = lambda i: (0, 0)
    row = lambda i: (i, 0)
    return pl.pallas_call(
        _merge_kernel,
        grid=(m // tm,),
        in_specs=[pl.BlockSpec((tm, d), row), pl.BlockSpec((tm, NSA_W), row),
                  pl.BlockSpec((tm, RW_W), row), pl.BlockSpec((tm, 2 * d), row),
                  pl.BlockSpec((NSA_W, d), full), pl.BlockSpec((RW_W, d), full),
                  pl.BlockSpec((d, d), full)],
        out_specs=pl.BlockSpec((tm, d), row),
        out_shape=jax.ShapeDtypeStruct((m, d), F32),
        compiler_params=_cparams("parallel"),
        name="merge_out",
    )(x, o_nsa, o_rw, u_mg, pa, pb, wo)


FFN_HALO = 16


def _ffn_kernel(x_ref, xp_ref, g_ref, wa_ref, wb_ref, ca_ref, cb_ref, wd_ref, o_ref, h_scr, *,
                tiles_per_seq):
    i = pl.program_id(0)
    j = pl.program_id(1)

    @pl.when(j == 0)
    def _():
        x = x_ref[...]
        h_scr[FFN_HALO:, :] = _rms(x, g_ref[...]).astype(BF16)
        hp = _rms(xp_ref[...], g_ref[...])
        h_scr[0:FFN_HALO, :] = jnp.where(i % tiles_per_seq == 0, 0.0, hp).astype(BF16)
        o_ref[...] = x

    h = h_scr[...]

    def conv(u, c_ref):
        cw = c_ref[...]
        return (cw[2:3] * u[FFN_HALO:] + cw[1:2] * pltpu.roll(u, 1, axis=0)[FFN_HALO:]
                + cw[0:1] * pltpu.roll(u, 2, axis=0)[FFN_HALO:])

    a = conv(_dot(h, wa_ref[...]), ca_ref)
    b = conv(_dot(h, wb_ref[...]), cb_ref)
    o_ref[...] += _dot((_silu(a) * b).astype(BF16), wd_ref[...])


def _ffn(x, gain, w_up, conv_w, w_down, seq, tm=1024, tf=1408):
    m, d = x.shape
    nf = D_FF // tf
    return pl.pallas_call(
        functools.partial(_ffn_kernel, tiles_per_seq=seq // tm),
        grid=(m // tm, nf),
        in_specs=[pl.BlockSpec((tm, d), lambda i, j: (i, 0)),
                  pl.BlockSpec((FFN_HALO, d),
                               lambda i, j: (jnp.maximum(i * (tm // FFN_HALO) - 1, 0), 0)),
                  pl.BlockSpec((1, d), lambda i, j: (0, 0)),
                  pl.BlockSpec((d, tf), lambda i, j: (0, j)),
                  pl.BlockSpec((d, tf), lambda i, j: (0, j + nf)),
                  pl.BlockSpec((3, tf), lambda i, j: (0, j)),
                  pl.BlockSpec((3, tf), lambda i, j: (0, j + nf)),
                  pl.BlockSpec((tf, d), lambda i, j: (j, 0))],
        out_specs=pl.BlockSpec((tm, d), lambda i, j: (i, 0)),
        out_shape=jax.ShapeDtypeStruct((m, d), F32),
        scratch_shapes=[pltpu.VMEM((tm + FFN_HALO, d), BF16)],
        compiler_params=_cparams("parallel", "arbitrary"),
        name="conv_ffn",
    )(x, x, gain, w_up, w_up, conv_w, conv_w, w_down)


def _split_w_in(w):
    o = 0
    q = w[:, o:o + NSA_W]; o += NSA_W
    kv = w[:, o:o + 6 * KV_W]; o += 6 * KV_W
    n_gate = NSA_HEADS * 3
    gl = w[:, o:o + n_gate]; o += n_gate
    rw = w[:, o:o + RW_COLS]; o += RW_COLS
    mg = w[:, o:]
    per_g = n_gate // NSA_KV_HEADS
    pad = jnp.zeros((w.shape[0], LANES - per_g), w.dtype)
    gates = [jnp.concatenate([gl[:, g * per_g:(g + 1) * per_g], pad], axis=1)
             for g in range(NSA_KV_HEADS)]
    w_nsa = jnp.concatenate([q, kv] + gates, axis=1)
    return w_nsa.astype(BF16), rw.astype(BF16), mg.astype(BF16)


def kernel(x, norm_mix, norm_ffn, w_in, qk_gain, cmp_pe, cmp_w1, cmp_w2, rwkv_mu, rwkv_w0, rwkv_w_up,
           rwkv_a0, rwkv_a_up, rwkv_g_up, rwkv_k_k, rwkv_k_a, rwkv_r_k, rwkv_ln_w, rwkv_ln_b, vres_v0,
           vres_v1, vres_v2, proj_nsa, proj_rwkv, w_out, ffn_up, ffn_conv, ffn_down):
    batch, seq, d = x.shape
    depth = w_in.shape[0]
    m = batch * seq
    assert seq % SLC_KV_TILE == 0 and seq // SLC_BLOCK <= LANES and seq >= WINDOW + Q_TILE
    xf = x.reshape(m, d)
    half = CMP_STRIDE * HEAD_DIM
    hi = lax.broadcasted_iota(jnp.int32, (RW_W, RW_W), 0) // HEAD_DIM
    hj = lax.broadcasted_iota(jnp.int32, (RW_W, RW_W), 1) // HEAD_DIM
    seg = (hi == hj).astype(BF16)
    v_first = None
    for l in range(depth):
        w_nsa, w_rw, w_mg = _split_w_in(w_in[l])
        u_nsa, u_rw, u_mg = _inproj(xf, norm_mix[l][None], w_nsa, w_rw, w_mg)

        qn, cmp_in, ksa, vs, kwn, vw = _nsa_prep(u_nsa, qk_gain[l], seq)
        z = cmp_in.reshape(2, NSA_KV_HEADS, batch, seq // CMP_STRIDE, half)
        w1 = cmp_w1[l]
        w1ab = jnp.concatenate([w1[:, :CMP_STRIDE].reshape(2, half, HEAD_DIM),
                                w1[:, CMP_STRIDE:].reshape(2, half, HEAD_DIM)], axis=2).astype(BF16)
        w1flat = w1.reshape(2, 2 * half, HEAD_DIM).astype(BF16)
        pe8 = jnp.broadcast_to(cmp_pe[l].reshape(2, 1, 2 * half), (2, 8, 2 * half))
        cmp_kv = _nsa_compress(z, w1ab, pe8, w1flat, cmp_w2[l].astype(BF16), qk_gain[l])
        o_nsa = _nsa_attn(qn, u_nsa, cmp_kv, ksa, vs, kwn, vw, batch, seq)

        zero = jnp.zeros((RW_W,), F32)
        vec = jnp.stack([rwkv_w0[l], rwkv_a0[l], rwkv_k_k[l], rwkv_k_a[l], rwkv_r_k[l].reshape(RW_W),
                         zero, zero, zero])
        vres = None
        if l > 0:
            vres = (v_first, vres_v0[l - 1][None], vres_v1[l - 1].astype(BF16),
                    vres_v2[l - 1].astype(BF16))
        r, lw, kh, v, kk, a, g, bonus = _rwkv_prep(
            u_rw, rwkv_mu[l][None], vec, rwkv_w_up[l].astype(BF16), rwkv_a_up[l].astype(BF16),
            rwkv_g_up[l].astype(BF16), seg, vres, seq)
        if l == 0:
            v_first = v
        o_rw = _rwkv_chunk(r, lw, kh, v, kk, a, g, bonus, rwkv_ln_w[l][None], rwkv_ln_b[l][None],
                           batch, seq, None)

        xf = _merge(xf, o_nsa, o_rw, u_mg, proj_nsa[l].astype(BF16), proj_rwkv[l].astype(BF16),
                    w_out[l].astype(BF16))
        xf = _ffn(xf, norm_ffn[l][None], ffn_up[l].astype(BF16), ffn_conv[l],
                  ffn_down[l].astype(BF16), seq)
    return xf.reshape(batch, seq, d)
```

```python
import functools

import jax
import jax.numpy as jnp
from jax import lax
from jax.experimental import pallas as pl
from jax.experimental.pallas import tpu as pltpu

F32 = jnp.float32
BF16 = jnp.bfloat16
HIGHEST = lax.Precision.HIGHEST

LANES = 128
VMEM_LIMIT_BYTES = 56 * 1024 * 1024

NSA_HEADS = 8
NSA_KV_HEADS = 2
NSA_GROUP = NSA_HEADS // NSA_KV_HEADS
HEAD_DIM = 64
CMP_BLOCK = 32
CMP_STRIDE = 16
SLC_BLOCK = 64
SLC_SHIFT = 6
SLC_TOPK = 16
WINDOW = 512
FORCE_SCORE = 1e4
NEG_INF = -1e30
LOG2_E = 1.4426950408889634
RWKV_HEADS = 8
DECAY_RANK = 64
ICLR_RANK = 64
GATE_RANK = 128
VRES_RANK = 32
LNX_EPS = 1e-5 * HEAD_DIM
D_FF = 2816
EPS = 1e-6

NSA_W = NSA_HEADS * HEAD_DIM
KV_W = NSA_KV_HEADS * HEAD_DIM
RW_W = RWKV_HEADS * HEAD_DIM
RW_COLS = 3 * RW_W + DECAY_RANK + ICLR_RANK + GATE_RANK
NSA_SLAB = NSA_W + 6 * KV_W + NSA_KV_HEADS * LANES

Q_TILE = 256
SLC_KV_TILE = 1024
RW_CHUNK = 64
RW_SUB = 16


def _cparams(*sem):
    return pltpu.CompilerParams(dimension_semantics=sem, vmem_limit_bytes=VMEM_LIMIT_BYTES)


def _dot(a, b, precision=None):
    return lax.dot_general(a, b, (((1,), (0,)), ((), ())), precision=precision,
                           preferred_element_type=F32)


def _dot_nt(a, b, precision=None):
    return lax.dot_general(a, b, (((1,), (1,)), ((), ())), precision=precision,
                           preferred_element_type=F32)


def _dot_tn(a, b, precision=None):
    return lax.dot_general(a, b, (((0,), (0,)), ((), ())), precision=precision,
                           preferred_element_type=F32)


def _rms(x, gain):
    return x * lax.rsqrt(jnp.mean(x * x, axis=-1, keepdims=True) + EPS) * gain


def _sigmoid(x):
    return 1.0 / (1.0 + jnp.exp(-x))


def _silu(x):
    return x * _sigmoid(x)


def _inproj_kernel(x_ref, g_ref, wn_ref, wr_ref, wm_ref, on_ref, or_ref, om_ref):
    h = _rms(x_ref[...], g_ref[...]).astype(BF16)
    on_ref[...] = _dot(h, wn_ref[...])
    or_ref[...] = _dot(h, wr_ref[...])
    om_ref[...] = _dot(h, wm_ref[...])


def _inproj(x, gain, w_nsa, w_rw, w_mg, tm=512):
    m, d = x.shape
    outs = [w_nsa.shape[1], w_rw.shape[1], w_mg.shape[1]]
    full = lambda i: (0, 0)
    return pl.pallas_call(
        _inproj_kernel,
        grid=(m // tm,),
        in_specs=[pl.BlockSpec((tm, d), lambda i: (i, 0)), pl.BlockSpec((1, d), full)]
        + [pl.BlockSpec((d, n), full, pipeline_mode=pl.Buffered(1)) for n in outs],
        out_specs=[pl.BlockSpec((tm, n), lambda i: (i, 0)) for n in outs],
        out_shape=[jax.ShapeDtypeStruct((m, n), F32) for n in outs],
        compiler_params=_cparams("parallel"),
        name="inproj",
    )(x, gain, w_nsa, w_rw, w_mg)


MASK_BIG = 2.0 ** 99
KS_AUG_W = LANES + HEAD_DIM


V_AUG_W = 2 * HEAD_DIM


def _nsa_prep_kernel(q_ref, kc_ref, vc_ref, ks_ref, vs_ref, kw_ref, vw_ref, gn_ref, qn_ref, cmp_ref,
                     ksa_ref, vs_o_ref, kwn_ref, vw_o_ref, *, tiles_per_seq):
    gn = gn_ref[...]
    tm = ks_ref.shape[0]
    pos = (pl.program_id(0) % tiles_per_seq) * tm + lax.broadcasted_iota(jnp.int32, (tm, LANES), 0)
    blk_col = lax.broadcasted_iota(jnp.int32, (tm, LANES), 1)
    onehot = jnp.where(jnp.right_shift(pos, SLC_SHIFT) == blk_col, -MASK_BIG, 0.0).astype(BF16)
    ones = jnp.ones((tm, HEAD_DIM), BF16)
    q_scale = HEAD_DIM ** -0.5 * LOG2_E
    for h in range(NSA_HEADS):
        qh = q_ref[:, h * HEAD_DIM:(h + 1) * HEAD_DIM]
        qn_ref[h // NSA_GROUP, h % NSA_GROUP] = (_rms(qh, gn[0:1]) * q_scale).astype(BF16)
    for g in range(NSA_KV_HEADS):
        sl = slice(g * HEAD_DIM, (g + 1) * HEAD_DIM)
        cmp_ref[0, g] = kc_ref[:, sl]
        cmp_ref[1, g] = vc_ref[:, sl]
        ksa_ref[g, :, :LANES] = onehot
        ksa_ref[g, :, LANES:] = _rms(ks_ref[:, sl], gn[2:3]).astype(BF16)
        kwn_ref[g] = _rms(kw_ref[:, sl], gn[3:4]).astype(BF16)
        vs_o_ref[g, :, :HEAD_DIM] = vs_ref[:, sl].astype(BF16)
        vs_o_ref[g, :, HEAD_DIM:] = ones
        vw_o_ref[g, :, :HEAD_DIM] = vw_ref[:, sl].astype(BF16)
        vw_o_ref[g, :, HEAD_DIM:] = ones


def _nsa_prep(u_nsa, qk_gain, seq, tm=512):
    m = u_nsa.shape[0]
    g = NSA_KV_HEADS
    col0 = NSA_W // KV_W
    in_specs = [pl.BlockSpec((tm, NSA_W), lambda i: (i, 0))]
    in_specs += [pl.BlockSpec((tm, KV_W), functools.partial(lambda i, c: (i, c), c=col0 + c))
                 for c in range(6)]
    in_specs.append(pl.BlockSpec((4, HEAD_DIM), lambda i: (0, 0)))
    spec3 = lambda w: pl.BlockSpec((g, tm, w), lambda i: (0, i, 0))
    shape3 = lambda w: jax.ShapeDtypeStruct((g, m, w), BF16)
    return pl.pallas_call(
        functools.partial(_nsa_prep_kernel, tiles_per_seq=seq // tm),
        grid=(m // tm,),
        in_specs=in_specs,
        out_specs=[pl.BlockSpec((g, NSA_GROUP, tm, HEAD_DIM), lambda i: (0, 0, i, 0)),
                   pl.BlockSpec((2, g, tm, HEAD_DIM), lambda i: (0, 0, i, 0)),
                   spec3(KS_AUG_W), spec3(V_AUG_W), spec3(HEAD_DIM), spec3(V_AUG_W)],
        out_shape=[jax.ShapeDtypeStruct((g, NSA_GROUP, m, HEAD_DIM), BF16),
                   jax.ShapeDtypeStruct((2, g, m, HEAD_DIM), F32),
                   shape3(KS_AUG_W), shape3(V_AUG_W), shape3(HEAD_DIM), shape3(V_AUG_W)],
        compiler_params=_cparams("parallel"),
        name="nsa_prep",
    )(*([u_nsa] * 7), qk_gain)


def _nsa_compress_kernel(z_ref, w1ab_ref, pe_ref, w1_ref, w2_ref, gn_ref, o_ref):
    kv = pl.program_id(0)
    nhalf = z_ref.shape[0]
    p = _dot(z_ref[...].astype(BF16), w1ab_ref[...])
    a = p[:, :HEAD_DIM]
    b_next = pltpu.roll(p[:, HEAD_DIM:], nhalf - 1, axis=0)
    c = _dot(pe_ref[...].astype(BF16), w1_ref[...])[0:1]
    hid = _silu(a + b_next + c)
    o = _dot(hid.astype(BF16), w2_ref[...])
    o_ref[:, :HEAD_DIM] = jnp.where(kv == 0, _rms(o, gn_ref[1:2]), o).astype(BF16)
    o_ref[:, HEAD_DIM:] = jnp.ones((nhalf, HEAD_DIM), BF16)


def _nsa_compress(z, w1ab, pe8, w1flat, w2, qk_gain):
    _, g, b, nhalf, zw = z.shape
    sq = lambda *shape: pl.BlockSpec((None,) + shape, lambda kv, gi, bi: (kv,) + (0,) * len(shape))
    return pl.pallas_call(
        _nsa_compress_kernel,
        grid=(2, g, b),
        in_specs=[pl.BlockSpec((None, None, None, nhalf, zw), lambda kv, gi, bi: (kv, gi, bi, 0, 0)),
                  sq(zw, 2 * HEAD_DIM), sq(8, 2 * zw), sq(2 * zw, HEAD_DIM), sq(HEAD_DIM, HEAD_DIM),
                  pl.BlockSpec((4, HEAD_DIM), lambda kv, gi, bi: (0, 0))],
        out_specs=pl.BlockSpec((None, None, None, nhalf, V_AUG_W),
                               lambda kv, gi, bi: (kv, gi, bi, 0, 0)),
        out_shape=jax.ShapeDtypeStruct((2, g, b, nhalf, V_AUG_W), BF16),
        compiler_params=_cparams("parallel", "parallel", "parallel"),
        name="nsa_compress",
    )(z, w1ab, pe8, w1flat, w2, qk_gain)


def _masked_attend(s, mask, v_aug):
    s = jnp.where(mask, s, NEG_INF)
    mx = jnp.max(s, axis=-1, keepdims=True)
    e = jnp.exp2(s - mx)
    ov = _dot(e.astype(BF16), v_aug)
    inv = jnp.where(mx > 0.5 * NEG_INF, 1.0 / ov[:, HEAD_DIM:HEAD_DIM + 1], 0.0)
    return e, ov[:, :HEAD_DIM] * inv, inv


def _nsa_attn_kernel(q_ref, gl_ref, kc_ref, vc_ref, ks_ref, vs_ref, kw_ref, vw_ref, o_ref, *, seq):
    i = pl.program_id(2)
    hg, tq = q_ref.shape[0], q_ref.shape[1]
    t0 = i * tq
    q = q_ref[...].reshape(hg * tq, HEAD_DIM)
    t_tok = t0 + lax.broadcasted_iota(jnp.int32, (tq, 1), 0)
    t_rows = jnp.concatenate([t_tok] * hg, axis=0)

    ncp = kc_ref.shape[0]
    s = _dot_nt(q, kc_ref[:, :HEAD_DIM])
    n_idx = lax.broadcasted_iota(jnp.int32, (1, ncp), 1)
    cmp_valid = (n_idx * CMP_STRIDE + (CMP_BLOCK - 1)) <= t_rows
    e_cmp, o_cmp, inv_cmp = _masked_attend(s, cmp_valid, vc_ref[...])

    nsp = LANES
    p_sum = e_cmp[0:tq] * inv_cmp[0:tq]
    for h in range(1, hg):
        p_sum = p_sum + e_cmp[h * tq:(h + 1) * tq] * inv_cmp[h * tq:(h + 1) * tq]
    ov_n = lax.broadcasted_iota(jnp.int32, (ncp, nsp), 0) * CMP_STRIDE
    ov_s = lax.broadcasted_iota(jnp.int32, (ncp, nsp), 1) * SLC_BLOCK
    overlap = jnp.where((ov_n < ov_s + SLC_BLOCK) & (ov_n + CMP_BLOCK > ov_s), 1.0, 0.0).astype(BF16)
    p_hi = p_sum.astype(BF16)
    r1 = p_sum - p_hi.astype(F32)
    p_mid = r1.astype(BF16)
    p_lo = (r1 - p_mid.astype(F32)).astype(BF16)
    imp3 = _dot(jnp.concatenate([p_hi, p_mid, p_lo], axis=0), overlap)
    imp = imp3[0:tq] + imp3[tq:2 * tq] + imp3[2 * tq:3 * tq]
    s_idx = lax.broadcasted_iota(jnp.int32, (1, nsp), 1)
    cur = jnp.right_shift(t_tok, SLC_SHIFT)
    forced = (s_idx == 0) | (s_idx == cur) | (s_idx == cur - 1)
    imp = jnp.where(forced, FORCE_SCORE, imp)
    imp = jnp.where(s_idx > cur, -1.0, imp)

    work = imp.T
    blk = lax.broadcasted_iota(jnp.int32, (nsp, tq), 0).astype(F32)
    unsel_t = jnp.ones((nsp, tq), F32)
    for _ in range(SLC_TOPK):
        mx = jnp.max(work, axis=0, keepdims=True)
        first = jnp.min(jnp.where(work == mx, blk, float(nsp)), axis=0, keepdims=True)
        chosen = blk == first
        unsel_t = jnp.where(chosen, 0.0, unsel_t)
        work = jnp.where(chosen, -3e38, work)
    unsel = unsel_t.T.astype(BF16)

    span = WINDOW + tq
    w0 = pl.multiple_of(jnp.maximum(t0 - WINDOW, 0), tq)
    kw = kw_ref[pl.ds(w0, span), :]
    vw = vw_ref[pl.ds(w0, span), :]
    sw = _dot_nt(q, kw)
    dist = t_rows - (w0 + lax.broadcasted_iota(jnp.int32, (1, span), 1))
    _, o_win, _ = _masked_attend(sw, (dist >= 0) & (dist < WINDOW), vw)

    tk = SLC_KV_TILE
    q_aug = jnp.concatenate([jnp.concatenate([unsel] * hg, axis=0), q], axis=1)

    def slc_tile(j, carry, causal):
        k0 = pl.multiple_of(j * tk, tk)
        sc = _dot_nt(q_aug, ks_ref[pl.ds(k0, tk), :])
        if causal:
            sc = jnp.where(k0 + lax.broadcasted_iota(jnp.int32, (1, tk), 1) <= t_rows, sc, NEG_INF)
        m_i, acc = carry
        m_new = jnp.maximum(m_i, jnp.max(sc, axis=-1, keepdims=True))
        p = jnp.exp2(sc - m_new)
        return m_new, jnp.exp2(m_i - m_new) * acc + _dot(p.astype(BF16), vs_ref[pl.ds(k0, tk), :])

    init = (jnp.full((hg * tq, 1), NEG_INF, F32), jnp.zeros((hg * tq, V_AUG_W), F32))
    j_diag = t0 // tk
    carry = lax.fori_loop(0, j_diag, lambda j, c: slc_tile(j, c, False), init)
    _, acc_s = slc_tile(j_diag, carry, True)
    o_slc = acc_s[:, :HEAD_DIM] / acc_s[:, HEAD_DIM:HEAD_DIM + 1]

    gates = _sigmoid(gl_ref[...])
    outs = []
    for h in range(hg):
        r = slice(h * tq, (h + 1) * tq)
        outs.append(gates[:, 3 * h:3 * h + 1] * o_cmp[r] + gates[:, 3 * h + 1:3 * h + 2] * o_slc[r]
                    + gates[:, 3 * h + 2:3 * h + 3] * o_win[r])
    o_ref[...] = jnp.concatenate(outs, axis=1)


def _nsa_attn(qn, u_nsa, cmp_kv, ksa, vs, kwn, vw, batch, seq):
    m = u_nsa.shape[0]
    g = NSA_KV_HEADS
    nq = seq // Q_TILE
    ncp = cmp_kv.shape[3]
    gw = NSA_GROUP * HEAD_DIM
    gate_col0 = (NSA_W + 6 * KV_W) // LANES
    cmp_spec = lambda kv: pl.BlockSpec((None, None, None, ncp, V_AUG_W),
                                       lambda b, gi, i: (kv, gi, b, 0, 0))
    seq_spec = lambda w: pl.BlockSpec((None, seq, w), lambda b, gi, i: (gi, b, 0))
    return pl.pallas_call(
        functools.partial(_nsa_attn_kernel, seq=seq),
        grid=(batch, g, nq),
        in_specs=[pl.BlockSpec((None, NSA_GROUP, Q_TILE, HEAD_DIM),
                               lambda b, gi, i: (gi, 0, b * nq + i, 0)),
                  pl.BlockSpec((Q_TILE, LANES), lambda b, gi, i: (b * nq + i, gate_col0 + gi)),
                  cmp_spec(0), cmp_spec(1), seq_spec(KS_AUG_W), seq_spec(V_AUG_W),
                  seq_spec(HEAD_DIM), seq_spec(V_AUG_W)],
        out_specs=pl.BlockSpec((Q_TILE, gw), lambda b, gi, i: (b * nq + i, gi)),
        out_shape=jax.ShapeDtypeStruct((m, NSA_W), F32),
        compiler_params=_cparams("parallel", "parallel", "arbitrary"),
        name="nsa_attn",
    )(qn, u_nsa, cmp_kv, cmp_kv, ksa, vs, kwn, vw)


def _rwkv_prep_kernel(*refs, tiles_per_seq, has_vres):
    if has_vres:
        (z_ref, zp_ref, mu_ref, vec_ref, wup_ref, aup_ref, gup_ref, seg_ref, vf_ref, v0_ref, v1_ref,
         v2_ref, r_o, lw_o, k_o, v_o, kk_o, a_o, g_o, bonus_o) = refs
    else:
        (z_ref, zp_ref, mu_ref, vec_ref, wup_ref, aup_ref, gup_ref, seg_ref,
         r_o, lw_o, k_o, v_o, kk_o, a_o, g_o, bonus_o) = refs
    i = pl.program_id(0)
    z = z_ref[...]
    tm = z.shape[0]
    prev_last = jnp.where(i % tiles_per_seq == 0, 0.0, zp_ref[7:8, :])
    row = lax.broadcasted_iota(jnp.int32, (tm, 1), 0)
    z_prev = jnp.where(row == 0, prev_last, pltpu.roll(z, 1, axis=0))
    z = z + mu_ref[...] * (z_prev - z)
    w = RW_W
    r = z[:, 0:w]
    k = z[:, w:2 * w]
    v = z[:, 2 * w:3 * w]
    wd = z[:, 3 * w:3 * w + DECAY_RANK]
    ad = z[:, 3 * w + DECAY_RANK:3 * w + DECAY_RANK + ICLR_RANK]
    gd = z[:, 3 * w + DECAY_RANK + ICLR_RANK:]
    vec = vec_ref[...]
    w0, a0, k_k, k_a, r_k = (vec[n:n + 1] for n in range(5))
    if has_vres:
        lo = _dot(_dot(v.astype(BF16), v1_ref[...]).astype(BF16), v2_ref[...])
        v = v + (vf_ref[...] - v) * _sigmoid(v0_ref[...] + lo)
    wl = w0 + _dot(jnp.tanh(wd).astype(BF16), wup_ref[...])
    neg = -wl
    softplus = jnp.maximum(neg, 0.0) + jnp.log(1.0 + jnp.exp(-jnp.abs(neg)))
    lw_o[...] = -jnp.exp(-softplus - 0.5)
    a = _sigmoid(a0 + _dot(ad.astype(BF16), aup_ref[...]))
    g_o[...] = _dot(_sigmoid(gd).astype(BF16), gup_ref[...])
    seg = seg_ref[...]
    kk = k * k_k
    kh = k * (1.0 + (a - 1.0) * k_a)
    x = jnp.concatenate([kk * kk, r * kh * r_k], axis=0)
    x_hi = x.astype(BF16)
    x_r = x - x_hi.astype(F32)
    x_mid = x_r.astype(BF16)
    x_lo = (x_r - x_mid.astype(F32)).astype(BF16)
    sums = _dot(jnp.concatenate([x_hi, x_mid, x_lo], axis=0), seg)
    sums = sums[0:2 * tm] + sums[2 * tm:4 * tm] + sums[4 * tm:6 * tm]
    kk_o[...] = kk / jnp.maximum(jnp.sqrt(sums[0:tm]), 1e-12)
    bonus_o[...] = sums[tm:2 * tm] * v
    r_o[...] = r
    k_o[...] = kh
    v_o[...] = v
    a_o[...] = a


def _rwkv_prep(u_rw, mu, vec, wup, aup, gup, seg, vres, seq, tm=256):
    m = u_rw.shape[0]
    full = lambda i: (0, 0)
    row = lambda i: (i, 0)
    in_specs = [pl.BlockSpec((tm, RW_COLS), row),
                pl.BlockSpec((8, RW_COLS), lambda i: (jnp.maximum(i * (tm // 8) - 1, 0), 0)),
                pl.BlockSpec((1, RW_COLS), full), pl.BlockSpec((8, RW_W), full),
                pl.BlockSpec((DECAY_RANK, RW_W), full), pl.BlockSpec((ICLR_RANK, RW_W), full),
                pl.BlockSpec((GATE_RANK, RW_W), full), pl.BlockSpec((RW_W, RW_W), full)]
    args = [u_rw, u_rw, mu, vec, wup, aup, gup, seg]
    if vres is not None:
        v_first, v0, v1, v2 = vres
        in_specs += [pl.BlockSpec((tm, RW_W), row), pl.BlockSpec((1, RW_W), full),
                     pl.BlockSpec((RW_W, VRES_RANK), full), pl.BlockSpec((VRES_RANK, RW_W), full)]
        args += [v_first, v0, v1, v2]
    return pl.pallas_call(
        functools.partial(_rwkv_prep_kernel, tiles_per_seq=seq // tm, has_vres=vres is not None),
        grid=(m // tm,),
        in_specs=in_specs,
        out_specs=[pl.BlockSpec((tm, RW_W), row)] * 8,
        out_shape=[jax.ShapeDtypeStruct((m, RW_W), F32)] * 8,
        compiler_params=_cparams("parallel"),
        name="rwkv_prep",
    )(*args)


def _b(x):
    return x.astype(BF16)


def _tri_inverse(l_list, eye):
    c = eye.shape[0]
    ri = lax.broadcasted_iota(jnp.int32, (c, c), 0)
    ci = lax.broadcasted_iota(jnp.int32, (c, c), 1)
    same_blk = (ri // RW_SUB) == (ci // RW_SUB)
    ld = [jnp.where(same_blk, l, 0.0) for l in l_list]
    lo = [_b(l - d0) for l, d0 in zip(l_list, ld)]
    d = [eye + x for x in ld]
    p = [_b(x) for x in ld]
    sq = 1
    while 2 * sq < RW_SUB:
        p = [_b(_dot(x, x)) for x in p]
        d = [x + _dot(_b(x), y) for x, y in zip(d, p)]
        sq *= 2
    d = [_b(x) for x in d]
    n = [_dot(x, y) for x, y in zip(d, lo)]
    x = [eye + y for y in n]
    n = [_b(y) for y in n]
    terms = 2
    while terms < c // RW_SUB:
        n = [_b(_dot(y, y)) for y in n]
        x = [z + _dot(_b(z), y) for z, y in zip(x, n)]
        terms *= 2
    return [_dot(_b(z), y) for z, y in zip(x, d)]


def _rwkv_chunk_kernel(r_ref, lw_ref, k_ref, v_ref, kk_ref, a_ref, g_ref, bonus_ref, lnw_ref,
                       lnb_ref, o_ref, state_ref, *, chain_prec):
    t = pl.program_id(2)

    @pl.when(t == 0)
    def _():
        state_ref[...] = jnp.zeros_like(state_ref)

    rt_rows = r_ref.shape[0]
    c = RW_CHUNK
    hd = HEAD_DIM
    nh = LANES // hd
    ri = lax.broadcasted_iota(jnp.int32, (c, c), 0)
    ci = lax.broadcasted_iota(jnp.int32, (c, c), 1)
    tril_incl = ri >= ci
    tril_strict = ri > ci
    is_diag = ri == ci
    ones_incl = jnp.where(tril_incl, 1.0, 0.0).astype(BF16)
    eye = jnp.where(is_diag, 1.0, 0.0)
    if chain_prec is None:
        chain = lambda x: _b(x)
    else:
        chain = lambda x: x

    nc = rt_rows // c
    rows = [slice(cb * c, (cb + 1) * c) for cb in range(nc)]
    lanes = [slice(hh * hd, (hh + 1) * hd) for hh in range(nh)]
    items = [(cb, hh) for cb in range(nc) for hh in range(nh)]

    def cumulative(lw):
        lw_hi = _b(lw)
        lw_r = lw - lw_hi.astype(F32)
        lw_mid = _b(lw_r)
        lw_lo = _b(lw_r - lw_mid.astype(F32))
        g3 = _dot(ones_incl, jnp.concatenate([lw_hi, lw_mid, lw_lo], axis=1))
        return g3[:, :LANES] + g3[:, LANES:2 * LANES] + g3[:, 2 * LANES:]

    lw = [lw_ref[rs, :] for rs in rows]
    gcum = [cumulative(x) for x in lw]
    g_last = [x[c - 1:c] for x in gcum]
    kk = [kk_ref[rs, :] for rs in rows]
    k = [k_ref[rs, :] for rs in rows]
    v2 = [v_ref[rs, :] for rs in rows]
    b = [kk[cb] * a_ref[rows[cb], :] for cb in range(nc)]
    e_neg = [jnp.exp(-x) for x in gcum]
    e_end = [jnp.exp(g_last[cb] - gcum[cb]) for cb in range(nc)]
    a_t2 = [(-kk[cb]) * jnp.exp(gcum[cb] - lw[cb]) for cb in range(nc)]
    r_t2 = [r_ref[rows[cb], :] * jnp.exp(gcum[cb]) for cb in range(nc)]
    b_t2 = [b[cb] * e_neg[cb] for cb in range(nc)]
    k_t2 = [k[cb] * e_neg[cb] for cb in range(nc)]
    b_end2 = [b[cb] * e_end[cb] for cb in range(nc)]
    k_end2 = [k[cb] * e_end[cb] for cb in range(nc)]
    decay2 = [jnp.exp(x) for x in g_last]

    gram = [_dot_nt(_b(jnp.concatenate([a_t2[cb][:, lanes[hh]], r_t2[cb][:, lanes[hh]]], axis=0)),
                    _b(jnp.concatenate([b_t2[cb][:, lanes[hh]], k_t2[cb][:, lanes[hh]]], axis=0)))
            for cb, hh in items]
    l_ab = [jnp.where(tril_strict, x[:c, :c], 0.0) for x in gram]
    m_rb = [_b(jnp.where(tril_incl, x[c:, :c], 0.0)) for x in gram]
    lm = [_b(jnp.concatenate([jnp.where(tril_strict, x[:c, c:], 0.0),
                              jnp.where(tril_incl, x[c:, c:], 0.0)], axis=0)) for x in gram]
    v_b = [_b(v2[cb][:, lanes[hh]]) for cb, hh in items]
    lmv = [_dot(x, y) for x, y in zip(lm, v_b)]
    tinv = _tri_inverse(l_ab, eye)
    tw = [_dot(_b(tinv[i]), _b(jnp.concatenate([a_t2[cb][:, lanes[hh]], lmv[i][:c]], axis=1)))
          for i, (cb, hh) in enumerate(items)]
    tw_b = [_b(x) for x in tw]
    mb = [_dot(x, y) for x, y in zip(m_rb, tw_b)]
    q_eff = [chain(r_t2[cb][:, lanes[hh]] + mb[i][:, :hd]) for i, (cb, hh) in enumerate(items)]
    y_loc = [mb[i][:, hd:] + lmv[i][c:] for i in range(len(items))]
    a_c = [chain(jnp.where(is_diag, decay2[cb][:, lanes[hh]], 0.0)
                 + _dot_tn(tw_b[i][:, :hd], _b(b_end2[cb][:, lanes[hh]])))
           for i, (cb, hh) in enumerate(items)]
    b_c = [_dot_tn(jnp.concatenate([tw_b[i][:, hd:], v_b[i]], axis=0),
                   _b(jnp.concatenate([b_end2[cb][:, lanes[hh]], k_end2[cb][:, lanes[hh]]], axis=0)))
           for i, (cb, hh) in enumerate(items)]

    states = [state_ref[hh] for hh in range(nh)]
    for i, (cb, hh) in enumerate(items):
        s0 = chain(states[hh])
        y = _dot_nt(q_eff[i], s0, chain_prec) + y_loc[i]
        states[hh] = _dot(s0, a_c[i], chain_prec) + b_c[i]
        rs, ls = rows[cb], lanes[hh]
        mean = jnp.mean(y, axis=-1, keepdims=True)
        yc = y - mean
        var = jnp.mean(yc * yc, axis=-1, keepdims=True)
        yn = yc * lax.rsqrt(var + LNX_EPS) * lnw_ref[:, ls] + lnb_ref[:, ls]
        o_ref[rs, ls] = (yn + bonus_ref[rs, ls]) * g_ref[rs, ls]
    for hh in range(nh):
        state_ref[hh] = states[hh]


def _rwkv_chunk(r, lw, k, v, kk, a, g, bonus, lnw, lnb, batch, seq, chain_prec, rt=1024):
    m = r.shape[0]
    nt = seq // rt
    pairs = RW_W // LANES
    row = pl.BlockSpec((rt, LANES), lambda b, p, t: (b * nt + t, p))
    vec = pl.BlockSpec((1, LANES), lambda b, p, t: (0, p))
    return pl.pallas_call(
        functools.partial(_rwkv_chunk_kernel, chain_prec=chain_prec),
        grid=(batch, pairs, nt),
        in_specs=[row] * 8 + [vec, vec],
        out_specs=row,
        out_shape=jax.ShapeDtypeStruct((m, RW_W), F32),
        scratch_shapes=[pltpu.VMEM((LANES // HEAD_DIM, HEAD_DIM, HEAD_DIM), F32)],
        compiler_params=_cparams("parallel", "parallel", "arbitrary"),
        name="rwkv_chunk",
    )(r, lw, k, v, kk, a, g, bonus, lnw, lnb)


def _merge_kernel(x_ref, on_ref, or_ref, mg_ref, pa_ref, pb_ref, wo_ref, o_ref):
    d = x_ref.shape[1]
    ya = _dot(on_ref[...].astype(BF16), pa_ref[...])
    yb = _dot(or_ref[...].astype(BF16), pb_ref[...])
    y = _sigmoid(mg_ref[:, :d]) * ya + _sigmoid(mg_ref[:, d:]) * yb
    o_ref[...] = x_ref[...] + _dot(y.astype(BF16), wo_ref[...])


def _merge(x, o_nsa, o_rw, u_mg, pa, pb, wo, tm=512):
    m, d = x.shape
    full = lambda i: (0, 0)
    row = lambda i: (i, 0)
    return pl.pallas_call(
        _merge_kernel,
        grid=(m // tm,),
        in_specs=[pl.BlockSpec((tm, d), row), pl.BlockSpec((tm, NSA_W), row),
                  pl.BlockSpec((tm, RW_W), row), pl.BlockSpec((tm, 2 * d), row),
                  pl.BlockSpec((NSA_W, d), full), pl.BlockSpec((RW_W, d), full),
                  pl.BlockSpec((d, d), full)],
        out_specs=pl.BlockSpec((tm, d), row),
        out_shape=jax.ShapeDtypeStruct((m, d), F32),
        compiler_params=_cparams("parallel"),
        name="merge_out",
    )(x, o_nsa, o_rw, u_mg, pa, pb, wo)


FFN_HALO = 16


def _ffn_kernel(x_ref, xp_ref, g_ref, wa_ref, wb_ref, ca_ref, cb_ref, wd_ref, o_ref, h_scr, *,
                tiles_per_seq):
    i = pl.program_id(0)
    j = pl.program_id(1)

    @pl.when(j == 0)
    def _():
        x = x_ref[...]
        h_scr[FFN_HALO:, :] = _rms(x, g_ref[...]).astype(BF16)
        hp = _rms(xp_ref[...], g_ref[...])
        h_scr[0:FFN_HALO, :] = jnp.where(i % tiles_per_seq == 0, 0.0, hp).astype(BF16)
        o_ref[...] = x

    h = h_scr[...]

    def conv(u, c_ref):
        cw = c_ref[...]
        return (cw[2:3] * u[FFN_HALO:] + cw[1:2] * pltpu.roll(u, 1, axis=0)[FFN_HALO:]
                + cw[0:1] * pltpu.roll(u, 2, axis=0)[FFN_HALO:])

    a = conv(_dot(h, wa_ref[...]), ca_ref)
    b = conv(_dot(h, wb_ref[...]), cb_ref)
    o_ref[...] += _dot((_silu(a) * b).astype(BF16), wd_ref[...])


def _ffn(x, gain, w_up, conv_w, w_down, seq, tm=512, tf=D_FF):
    m, d = x.shape
    nf = D_FF // tf
    wmode = dict(pipeline_mode=pl.Buffered(1)) if nf == 1 else {}
    return pl.pallas_call(
        functools.partial(_ffn_kernel, tiles_per_seq=seq // tm),
        grid=(m // tm, nf),
        in_specs=[pl.BlockSpec((tm, d), lambda i, j: (i, 0)),
                  pl.BlockSpec((FFN_HALO, d),
                               lambda i, j: (jnp.maximum(i * (tm // FFN_HALO) - 1, 0), 0)),
                  pl.BlockSpec((1, d), lambda i, j: (0, 0)),
                  pl.BlockSpec((d, tf), lambda i, j: (0, j), **wmode),
                  pl.BlockSpec((d, tf), lambda i, j: (0, j + nf), **wmode),
                  pl.BlockSpec((3, tf), lambda i, j: (0, j)),
                  pl.BlockSpec((3, tf), lambda i, j: (0, j + nf)),
                  pl.BlockSpec((tf, d), lambda i, j: (j, 0), **wmode)],
        out_specs=pl.BlockSpec((tm, d), lambda i, j: (i, 0)),
        out_shape=jax.ShapeDtypeStruct((m, d), F32),
        scratch_shapes=[pltpu.VMEM((tm + FFN_HALO, d), BF16)],
        compiler_params=_cparams("parallel", "arbitrary"),
        name="conv_ffn",
    )(x, x, gain, w_up, w_up, conv_w, conv_w, w_down)


def _split_w_in(w):
    o = 0
    q = w[:, o:o + NSA_W]; o += NSA_W
    kv = w[:, o:o + 6 * KV_W]; o += 6 * KV_W
    n_gate = NSA_HEADS * 3
    gl = w[:, o:o + n_gate]; o += n_gate
    rw = w[:, o:o + RW_COLS]; o += RW_COLS
    mg = w[:, o:]
    per_g = n_gate // NSA_KV_HEADS
    pad = jnp.zeros((w.shape[0], LANES - per_g), w.dtype)
    gates = [jnp.concatenate([gl[:, g * per_g:(g + 1) * per_g], pad], axis=1)
             for g in range(NSA_KV_HEADS)]
    w_nsa = jnp.concatenate([q, kv] + gates, axis=1)
    return w_nsa.astype(BF16), rw.astype(BF16), mg.astype(BF16)


def kernel(x, norm_mix, norm_ffn, w_in, qk_gain, cmp_pe, cmp_w1, cmp_w2, rwkv_mu, rwkv_w0, rwkv_w_up,
           rwkv_a0, rwkv_a_up, rwkv_g_up, rwkv_k_k, rwkv_k_a, rwkv_r_k, rwkv_ln_w, rwkv_ln_b, vres_v0,
           vres_v1, vres_v2, proj_nsa, proj_rwkv, w_out, ffn_up, ffn_conv, ffn_down):
    batch, seq, d = x.shape
    depth = w_in.shape[0]
    m = batch * seq
    assert seq % SLC_KV_TILE == 0 and seq // SLC_BLOCK <= LANES and seq >= WINDOW + Q_TILE
    xf = x.reshape(m, d)
    half = CMP_STRIDE * HEAD_DIM
    hi = lax.broadcasted_iota(jnp.int32, (RW_W, RW_W), 0) // HEAD_DIM
    hj = lax.broadcasted_iota(jnp.int32, (RW_W, RW_W), 1) // HEAD_DIM
    seg = (hi == hj).astype(BF16)
    v_first = None
    for l in range(depth):
        w_nsa, w_rw, w_mg = _split_w_in(w_in[l])
        u_nsa, u_rw, u_mg = _inproj(xf, norm_mix[l][None], w_nsa, w_rw, w_mg)

        qn, cmp_in, ksa, vs, kwn, vw = _nsa_prep(u_nsa, qk_gain[l], seq)
        z = cmp_in.reshape(2, NSA_KV_HEADS, batch, seq // CMP_STRIDE, half)
        w1 = cmp_w1[l]
        w1ab = jnp.concatenate([w1[:, :CMP_STRIDE].reshape(2, half, HEAD_DIM),
                                w1[:, CMP_STRIDE:].reshape(2, half, HEAD_DIM)], axis=2).astype(BF16)
        w1flat = w1.reshape(2, 2 * half, HEAD_DIM).astype(BF16)
        pe8 = jnp.broadcast_to(cmp_pe[l].reshape(2, 1, 2 * half), (2, 8, 2 * half))
        cmp_kv = _nsa_compress(z, w1ab, pe8, w1flat, cmp_w2[l].astype(BF16), qk_gain[l])
        o_nsa = _nsa_attn(qn, u_nsa, cmp_kv, ksa, vs, kwn, vw, batch, seq)

        zero = jnp.zeros((RW_W,), F32)
        vec = jnp.stack([rwkv_w0[l], rwkv_a0[l], rwkv_k_k[l], rwkv_k_a[l], rwkv_r_k[l].reshape(RW_W),
                         zero, zero, zero])
        vres = None
        if l > 0:
            vres = (v_first, vres_v0[l - 1][None], vres_v1[l - 1].astype(BF16),
                    vres_v2[l - 1].astype(BF16))
        r, lw, kh, v, kk, a, g, bonus = _rwkv_prep(
            u_rw, rwkv_mu[l][None], vec, rwkv_w_up[l].astype(BF16), rwkv_a_up[l].astype(BF16),
            rwkv_g_up[l].astype(BF16), seg, vres, seq)
        if l == 0:
            v_first = v
        o_rw = _rwkv_chunk(r, lw, kh, v, kk, a, g, bonus, rwkv_ln_w[l][None], rwkv_ln_b[l][None],
                           batch, seq, None)

        xf = _merge(xf, o_nsa, o_rw, u_mg, proj_nsa[l].astype(BF16), proj_rwkv[l].astype(BF16),
                    w_out[l].astype(BF16))
        xf = _ffn(xf, norm_ffn[l][None], ffn_up[l].astype(BF16), ffn_conv[l],
                  ffn_down[l].astype(BF16), seq)
    return xf.reshape(batch, seq, d)
```

```python
import functools

import jax
import jax.numpy as jnp
from jax import lax
from jax.experimental import pallas as pl
from jax.experimental.pallas import tpu as pltpu

F32 = jnp.float32
BF16 = jnp.bfloat16

LANES = 128
VMEM_LIMIT_BYTES = 56 * 1024 * 1024

NSA_HEADS = 8
NSA_KV_HEADS = 2
NSA_GROUP = NSA_HEADS // NSA_KV_HEADS
HEAD_DIM = 64
CMP_BLOCK = 32
CMP_STRIDE = 16
SLC_BLOCK = 64
SLC_SHIFT = 6
SLC_TOPK = 16
N_FORCED = 3
WINDOW = 512
FORCE_SCORE = 1e4
NEG_INF = -1e30
LOG2_E = 1.4426950408889634
RWKV_HEADS = 8
DECAY_RANK = 64
ICLR_RANK = 64
GATE_RANK = 128
VRES_RANK = 32
LNX_EPS = 1e-5 * HEAD_DIM
D_FF = 2816
EPS = 1e-6

NSA_W = NSA_HEADS * HEAD_DIM
KV_W = NSA_KV_HEADS * HEAD_DIM
RW_W = RWKV_HEADS * HEAD_DIM
RW_COLS = 3 * RW_W + DECAY_RANK + ICLR_RANK + GATE_RANK
NSA_SLAB = NSA_W + 6 * KV_W + NSA_KV_HEADS * LANES

Q_TILE = 256
SLC_KV_TILE = 1024
RW_CHUNK = 64
RW_SUB = 16


def _cparams(*sem):
    return pltpu.CompilerParams(dimension_semantics=sem, vmem_limit_bytes=VMEM_LIMIT_BYTES)


def _dot(a, b):
    return lax.dot_general(a, b, (((1,), (0,)), ((), ())), preferred_element_type=F32)


def _dot_nt(a, b):
    return lax.dot_general(a, b, (((1,), (1,)), ((), ())), preferred_element_type=F32)


def _dot_tn(a, b):
    return lax.dot_general(a, b, (((0,), (0,)), ((), ())), preferred_element_type=F32)


def _rms(x, gain):
    return x * lax.rsqrt(jnp.mean(x * x, axis=-1, keepdims=True) + EPS) * gain


def _sigmoid(x):
    return 1.0 / (1.0 + jnp.exp(-x))


def _silu(x):
    return x * _sigmoid(x)


def _inproj_kernel(x_ref, g_ref, wn_ref, wr_ref, wm_ref, on_ref, or_ref, om_ref):
    h = _rms(x_ref[...], g_ref[...]).astype(BF16)
    on_ref[...] = _dot(h, wn_ref[...])
    or_ref[...] = _dot(h, wr_ref[...])
    om_ref[...] = _dot(h, wm_ref[...])


def _inproj(x, gain, w_nsa, w_rw, w_mg, tm=512):
    m, d = x.shape
    outs = [w_nsa.shape[1], w_rw.shape[1], w_mg.shape[1]]
    full = lambda i: (0, 0)
    return pl.pallas_call(
        _inproj_kernel,
        grid=(m // tm,),
        in_specs=[pl.BlockSpec((tm, d), lambda i: (i, 0)), pl.BlockSpec((1, d), full)]
        + [pl.BlockSpec((d, n), full, pipeline_mode=pl.Buffered(1)) for n in outs],
        out_specs=[pl.BlockSpec((tm, n), lambda i: (i, 0)) for n in outs],
        out_shape=[jax.ShapeDtypeStruct((m, n), F32) for n in outs],
        compiler_params=_cparams("parallel"),
        name="inproj",
    )(x, gain, w_nsa, w_rw, w_mg)


MASK_BIG = 2.0 ** 99
KS_AUG_W = LANES + HEAD_DIM


V_AUG_W = 2 * HEAD_DIM


def _nsa_prep_kernel(q_ref, kc_ref, vc_ref, ks_ref, vs_ref, kw_ref, vw_ref, gn_ref, qn_ref, cmp_ref,
                     ksa_ref, vs_o_ref, kwn_ref, vw_o_ref, *, tiles_per_seq):
    gn = gn_ref[...]
    tm = ks_ref.shape[0]
    pos = (pl.program_id(0) % tiles_per_seq) * tm + lax.broadcasted_iota(jnp.int32, (tm, LANES), 0)
    blk_col = lax.broadcasted_iota(jnp.int32, (tm, LANES), 1)
    onehot = jnp.where(jnp.right_shift(pos, SLC_SHIFT) == blk_col, -MASK_BIG, 0.0).astype(BF16)
    ones = jnp.ones((tm, HEAD_DIM), BF16)
    q_scale = HEAD_DIM ** -0.5 * LOG2_E
    for h in range(NSA_HEADS):
        qh = q_ref[:, h * HEAD_DIM:(h + 1) * HEAD_DIM]
        qn_ref[h // NSA_GROUP, h % NSA_GROUP] = (_rms(qh, gn[0:1]) * q_scale).astype(BF16)
    for g in range(NSA_KV_HEADS):
        sl = slice(g * HEAD_DIM, (g + 1) * HEAD_DIM)
        cmp_ref[0, g] = kc_ref[:, sl]
        cmp_ref[1, g] = vc_ref[:, sl]
        ksa_ref[g, :, :LANES] = onehot
        ksa_ref[g, :, LANES:] = _rms(ks_ref[:, sl], gn[2:3]).astype(BF16)
        kwn_ref[g] = _rms(kw_ref[:, sl], gn[3:4]).astype(BF16)
        vs_o_ref[g, :, :HEAD_DIM] = vs_ref[:, sl].astype(BF16)
        vs_o_ref[g, :, HEAD_DIM:] = ones
        vw_o_ref[g, :, :HEAD_DIM] = vw_ref[:, sl].astype(BF16)
        vw_o_ref[g, :, HEAD_DIM:] = ones


def _nsa_prep(u_nsa, qk_gain, seq, tm=512):
    m = u_nsa.shape[0]
    g = NSA_KV_HEADS
    col0 = NSA_W // KV_W
    in_specs = [pl.BlockSpec((tm, NSA_W), lambda i: (i, 0))]
    in_specs += [pl.BlockSpec((tm, KV_W), functools.partial(lambda i, c: (i, c), c=col0 + c))
                 for c in range(6)]
    in_specs.append(pl.BlockSpec((4, HEAD_DIM), lambda i: (0, 0)))
    spec3 = lambda w: pl.BlockSpec((g, tm, w), lambda i: (0, i, 0))
    shape3 = lambda w: jax.ShapeDtypeStruct((g, m, w), BF16)
    return pl.pallas_call(
        functools.partial(_nsa_prep_kernel, tiles_per_seq=seq // tm),
        grid=(m // tm,),
        in_specs=in_specs,
        out_specs=[pl.BlockSpec((g, NSA_GROUP, tm, HEAD_DIM), lambda i: (0, 0, i, 0)),
                   pl.BlockSpec((2, g, tm, HEAD_DIM), lambda i: (0, 0, i, 0)),
                   spec3(KS_AUG_W), spec3(V_AUG_W), spec3(HEAD_DIM), spec3(V_AUG_W)],
        out_shape=[jax.ShapeDtypeStruct((g, NSA_GROUP, m, HEAD_DIM), BF16),
                   jax.ShapeDtypeStruct((2, g, m, HEAD_DIM), F32),
                   shape3(KS_AUG_W), shape3(V_AUG_W), shape3(HEAD_DIM), shape3(V_AUG_W)],
        compiler_params=_cparams("parallel"),
        name="nsa_prep",
    )(*([u_nsa] * 7), qk_gain)


def _nsa_compress_kernel(z_ref, w1ab_ref, pe_ref, w1_ref, w2_ref, gn_ref, o_ref):
    kv = pl.program_id(0)
    nhalf = z_ref.shape[0]
    p = _dot(z_ref[...].astype(BF16), w1ab_ref[...])
    a = p[:, :HEAD_DIM]
    b_next = pltpu.roll(p[:, HEAD_DIM:], nhalf - 1, axis=0)
    c = _dot(pe_ref[...].astype(BF16), w1_ref[...])[0:1]
    hid = _silu(a + b_next + c)
    o = _dot(hid.astype(BF16), w2_ref[...])
    o_ref[:, :HEAD_DIM] = jnp.where(kv == 0, _rms(o, gn_ref[1:2]), o).astype(BF16)
    o_ref[:, HEAD_DIM:] = jnp.ones((nhalf, HEAD_DIM), BF16)


def _nsa_compress(z, w1ab, pe8, w1flat, w2, qk_gain):
    _, g, b, nhalf, zw = z.shape
    sq = lambda *shape: pl.BlockSpec((None,) + shape, lambda kv, gi, bi: (kv,) + (0,) * len(shape))
    return pl.pallas_call(
        _nsa_compress_kernel,
        grid=(2, g, b),
        in_specs=[pl.BlockSpec((None, None, None, nhalf, zw), lambda kv, gi, bi: (kv, gi, bi, 0, 0)),
                  sq(zw, 2 * HEAD_DIM), sq(8, 2 * zw), sq(2 * zw, HEAD_DIM), sq(HEAD_DIM, HEAD_DIM),
                  pl.BlockSpec((4, HEAD_DIM), lambda kv, gi, bi: (0, 0))],
        out_specs=pl.BlockSpec((None, None, None, nhalf, V_AUG_W),
                               lambda kv, gi, bi: (kv, gi, bi, 0, 0)),
        out_shape=jax.ShapeDtypeStruct((2, g, b, nhalf, V_AUG_W), BF16),
        compiler_params=_cparams("parallel", "parallel", "parallel"),
        name="nsa_compress",
    )(z, w1ab, pe8, w1flat, w2, qk_gain)


def _masked_attend(s, mask, v_aug):
    s = jnp.where(mask, s, NEG_INF)
    mx = jnp.max(s, axis=-1, keepdims=True)
    e = jnp.exp2(s - mx)
    ov = _dot(e.astype(BF16), v_aug)
    inv = jnp.where(mx > 0.5 * NEG_INF, 1.0 / ov[:, HEAD_DIM:HEAD_DIM + 1], 0.0)
    return e, ov[:, :HEAD_DIM] * inv, inv


def _nsa_attn_kernel(q_ref, gl_ref, kc_ref, vc_ref, ks_ref, vs_ref, kw_ref, vw_ref, o_ref):
    i = pl.program_id(2)
    hg, tq = q_ref.shape[0], q_ref.shape[1]
    t0 = i * tq
    q = q_ref[...].reshape(hg * tq, HEAD_DIM)
    t_tok = t0 + lax.broadcasted_iota(jnp.int32, (tq, 1), 0)
    t_rows = jnp.concatenate([t_tok] * hg, axis=0)

    ncp = kc_ref.shape[0]
    s = _dot_nt(q, kc_ref[:, :HEAD_DIM])
    n_idx = lax.broadcasted_iota(jnp.int32, (1, ncp), 1)
    cmp_valid = (n_idx * CMP_STRIDE + (CMP_BLOCK - 1)) <= t_rows
    e_cmp, o_cmp, inv_cmp = _masked_attend(s, cmp_valid, vc_ref[...])

    nsp = LANES
    p_sum = e_cmp[0:tq] * inv_cmp[0:tq]
    for h in range(1, hg):
        p_sum = p_sum + e_cmp[h * tq:(h + 1) * tq] * inv_cmp[h * tq:(h + 1) * tq]
    ov_n = lax.broadcasted_iota(jnp.int32, (ncp, nsp), 0) * CMP_STRIDE
    ov_s = lax.broadcasted_iota(jnp.int32, (ncp, nsp), 1) * SLC_BLOCK
    overlap = jnp.where((ov_n < ov_s + SLC_BLOCK) & (ov_n + CMP_BLOCK > ov_s), 1.0, 0.0).astype(BF16)
    p_hi = p_sum.astype(BF16)
    r1 = p_sum - p_hi.astype(F32)
    p_mid = r1.astype(BF16)
    p_lo = (r1 - p_mid.astype(F32)).astype(BF16)
    imp3 = _dot(jnp.concatenate([p_hi, p_mid, p_lo], axis=0), overlap)
    imp = imp3[0:tq] + imp3[tq:2 * tq] + imp3[2 * tq:3 * tq]
    s_idx = lax.broadcasted_iota(jnp.int32, (1, nsp), 1)
    cur = jnp.right_shift(t_tok, SLC_SHIFT)
    forced = (s_idx == 0) | (s_idx == cur) | (s_idx == cur - 1)
    imp = jnp.where(forced, FORCE_SCORE, imp)
    imp = jnp.where(s_idx > cur, -1.0, imp)

    work = imp.T
    blk = lax.broadcasted_iota(jnp.int32, (nsp, tq), 0).astype(F32)
    forced_t = work >= FORCE_SCORE
    unsel_t = jnp.where(forced_t, 0.0, 1.0)
    work = jnp.where(forced_t, -3e38, work)
    for _ in range(SLC_TOPK - N_FORCED):
        mx = jnp.max(work, axis=0, keepdims=True)
        first = jnp.min(jnp.where(work == mx, blk, float(nsp)), axis=0, keepdims=True)
        chosen = blk == first
        unsel_t = jnp.where(chosen, 0.0, unsel_t)
        work = jnp.where(chosen, -3e38, work)
    unsel = unsel_t.T.astype(BF16)

    span = WINDOW + tq
    w0 = pl.multiple_of(jnp.maximum(t0 - WINDOW, 0), tq)
    kw = kw_ref[pl.ds(w0, span), :]
    vw = vw_ref[pl.ds(w0, span), :]
    sw = _dot_nt(q, kw)
    dist = t_rows - (w0 + lax.broadcasted_iota(jnp.int32, (1, span), 1))
    _, o_win, _ = _masked_attend(sw, (dist >= 0) & (dist < WINDOW), vw)

    tk = SLC_KV_TILE
    q_aug = jnp.concatenate([jnp.concatenate([unsel] * hg, axis=0), q], axis=1)

    def slc_tile(j, carry, causal):
        k0 = pl.multiple_of(j * tk, tk)
        sc = _dot_nt(q_aug, ks_ref[pl.ds(k0, tk), :])
        if causal:
            sc = jnp.where(k0 + lax.broadcasted_iota(jnp.int32, (1, tk), 1) <= t_rows, sc, NEG_INF)
        m_i, acc = carry
        m_new = jnp.maximum(m_i, jnp.max(sc, axis=-1, keepdims=True))
        p = jnp.exp2(sc - m_new)
        return m_new, jnp.exp2(m_i - m_new) * acc + _dot(p.astype(BF16), vs_ref[pl.ds(k0, tk), :])

    init = (jnp.full((hg * tq, 1), NEG_INF, F32), jnp.zeros((hg * tq, V_AUG_W), F32))
    j_diag = t0 // tk
    carry = lax.fori_loop(0, j_diag, lambda j, c: slc_tile(j, c, False), init)
    _, acc_s = slc_tile(j_diag, carry, True)
    o_slc = acc_s[:, :HEAD_DIM] / acc_s[:, HEAD_DIM:HEAD_DIM + 1]

    gates = _sigmoid(gl_ref[...])
    outs = []
    for h in range(hg):
        r = slice(h * tq, (h + 1) * tq)
        outs.append(gates[:, 3 * h:3 * h + 1] * o_cmp[r] + gates[:, 3 * h + 1:3 * h + 2] * o_slc[r]
                    + gates[:, 3 * h + 2:3 * h + 3] * o_win[r])
    o_ref[...] = jnp.concatenate(outs, axis=1)


def _nsa_attn(qn, u_nsa, cmp_kv, ksa, vs, kwn, vw, batch, seq):
    m = u_nsa.shape[0]
    g = NSA_KV_HEADS
    nq = seq // Q_TILE
    ncp = cmp_kv.shape[3]
    gw = NSA_GROUP * HEAD_DIM
    gate_col0 = (NSA_W + 6 * KV_W) // LANES
    cmp_spec = lambda kv: pl.BlockSpec((None, None, None, ncp, V_AUG_W),
                                       lambda b, gi, i: (kv, gi, b, 0, 0))
    seq_spec = lambda w: pl.BlockSpec((None, seq, w), lambda b, gi, i: (gi, b, 0))
    return pl.pallas_call(
        _nsa_attn_kernel,
        grid=(batch, g, nq),
        in_specs=[pl.BlockSpec((None, NSA_GROUP, Q_TILE, HEAD_DIM),
                               lambda b, gi, i: (gi, 0, b * nq + i, 0)),
                  pl.BlockSpec((Q_TILE, LANES), lambda b, gi, i: (b * nq + i, gate_col0 + gi)),
                  cmp_spec(0), cmp_spec(1), seq_spec(KS_AUG_W), seq_spec(V_AUG_W),
                  seq_spec(HEAD_DIM), seq_spec(V_AUG_W)],
        out_specs=pl.BlockSpec((Q_TILE, gw), lambda b, gi, i: (b * nq + i, gi)),
        out_shape=jax.ShapeDtypeStruct((m, NSA_W), F32),
        compiler_params=_cparams("parallel", "parallel", "arbitrary"),
        name="nsa_attn",
    )(qn, u_nsa, cmp_kv, cmp_kv, ksa, vs, kwn, vw)


def _rwkv_prep_kernel(*refs, tiles_per_seq, has_vres):
    if has_vres:
        (z_ref, zp_ref, mu_ref, vec_ref, wup_ref, aup_ref, gup_ref, seg_ref, vf_ref, v0_ref, v1_ref,
         v2_ref, r_o, lw_o, k_o, v_o, kk_o, a_o, g_o, bonus_o) = refs
    else:
        (z_ref, zp_ref, mu_ref, vec_ref, wup_ref, aup_ref, gup_ref, seg_ref,
         r_o, lw_o, k_o, v_o, kk_o, a_o, g_o, bonus_o) = refs
    i = pl.program_id(0)
    z = z_ref[...]
    tm = z.shape[0]
    prev_last = jnp.where(i % tiles_per_seq == 0, 0.0, zp_ref[7:8, :])
    row = lax.broadcasted_iota(jnp.int32, (tm, 1), 0)
    z_prev = jnp.where(row == 0, prev_last, pltpu.roll(z, 1, axis=0))
    z = z + mu_ref[...] * (z_prev - z)
    w = RW_W
    r = z[:, 0:w]
    k = z[:, w:2 * w]
    v = z[:, 2 * w:3 * w]
    wd = z[:, 3 * w:3 * w + DECAY_RANK]
    ad = z[:, 3 * w + DECAY_RANK:3 * w + DECAY_RANK + ICLR_RANK]
    gd = z[:, 3 * w + DECAY_RANK + ICLR_RANK:]
    vec = vec_ref[...]
    w0, a0, k_k, k_a, r_k = (vec[n:n + 1] for n in range(5))
    if has_vres:
        lo = _dot(_dot(v.astype(BF16), v1_ref[...]).astype(BF16), v2_ref[...])
        v = v + (vf_ref[...] - v) * _sigmoid(v0_ref[...] + lo)
    wl = w0 + _dot(jnp.tanh(wd).astype(BF16), wup_ref[...])
    neg = -wl
    softplus = jnp.maximum(neg, 0.0) + jnp.log(1.0 + jnp.exp(-jnp.abs(neg)))
    lw_o[...] = -jnp.exp(-softplus - 0.5)
    a = _sigmoid(a0 + _dot(ad.astype(BF16), aup_ref[...]))
    g_o[...] = _dot(_sigmoid(gd).astype(BF16), gup_ref[...])
    seg = seg_ref[...]
    kk = k * k_k
    kh = k * (1.0 + (a - 1.0) * k_a)
    x = jnp.concatenate([kk * kk, r * kh * r_k], axis=0)
    x_hi = x.astype(BF16)
    x_r = x - x_hi.astype(F32)
    x_mid = x_r.astype(BF16)
    x_lo = (x_r - x_mid.astype(F32)).astype(BF16)
    sums = _dot(jnp.concatenate([x_hi, x_mid, x_lo], axis=0), seg)
    sums = sums[0:2 * tm] + sums[2 * tm:4 * tm] + sums[4 * tm:6 * tm]
    kk_o[...] = kk / jnp.maximum(jnp.sqrt(sums[0:tm]), 1e-12)
    bonus_o[...] = sums[tm:2 * tm] * v
    r_o[...] = r
    k_o[...] = kh
    v_o[...] = v
    a_o[...] = a


def _rwkv_prep(u_rw, mu, vec, wup, aup, gup, seg, vres, seq, tm=256):
    m = u_rw.shape[0]
    full = lambda i: (0, 0)
    row = lambda i: (i, 0)
    in_specs = [pl.BlockSpec((tm, RW_COLS), row),
                pl.BlockSpec((8, RW_COLS), lambda i: (jnp.maximum(i * (tm // 8) - 1, 0), 0)),
                pl.BlockSpec((1, RW_COLS), full), pl.BlockSpec((8, RW_W), full),
                pl.BlockSpec((DECAY_RANK, RW_W), full), pl.BlockSpec((ICLR_RANK, RW_W), full),
                pl.BlockSpec((GATE_RANK, RW_W), full), pl.BlockSpec((RW_W, RW_W), full)]
    args = [u_rw, u_rw, mu, vec, wup, aup, gup, seg]
    if vres is not None:
        v_first, v0, v1, v2 = vres
        in_specs += [pl.BlockSpec((tm, RW_W), row), pl.BlockSpec((1, RW_W), full),
                     pl.BlockSpec((RW_W, VRES_RANK), full), pl.BlockSpec((VRES_RANK, RW_W), full)]
        args += [v_first, v0, v1, v2]
    return pl.pallas_call(
        functools.partial(_rwkv_prep_kernel, tiles_per_seq=seq // tm, has_vres=vres is not None),
        grid=(m // tm,),
        in_specs=in_specs,
        out_specs=[pl.BlockSpec((tm, RW_W), row)] * 8,
        out_shape=[jax.ShapeDtypeStruct((m, RW_W), F32)] * 8,
        compiler_params=_cparams("parallel"),
        name="rwkv_prep",
    )(*args)


def _b(x):
    return x.astype(BF16)


def _tri_inverse(l_list, eye):
    c = eye.shape[0]
    ri = lax.broadcasted_iota(jnp.int32, (c, c), 0)
    ci = lax.broadcasted_iota(jnp.int32, (c, c), 1)
    same_blk = (ri // RW_SUB) == (ci // RW_SUB)
    ld = [jnp.where(same_blk, l, 0.0) for l in l_list]
    lo = [_b(l - d0) for l, d0 in zip(l_list, ld)]
    d = [eye + x for x in ld]
    p = [_b(x) for x in ld]
    sq = 1
    while 2 * sq < RW_SUB:
        p = [_b(_dot(x, x)) for x in p]
        d = [x + _dot(_b(x), y) for x, y in zip(d, p)]
        sq *= 2
    d = [_b(x) for x in d]
    n = [_dot(x, y) for x, y in zip(d, lo)]
    x = [eye + y for y in n]
    n = [_b(y) for y in n]
    terms = 2
    while terms < c // RW_SUB:
        n = [_b(_dot(y, y)) for y in n]
        x = [z + _dot(_b(z), y) for z, y in zip(x, n)]
        terms *= 2
    return [_dot(_b(z), y) for z, y in zip(x, d)]


def _rwkv_chunk_kernel(r_ref, lw_ref, k_ref, v_ref, kk_ref, a_ref, g_ref, bonus_ref, lnw_ref,
                       lnb_ref, o_ref, state_ref):
    t = pl.program_id(2)

    @pl.when(t == 0)
    def _():
        state_ref[...] = jnp.zeros_like(state_ref)

    rt_rows = r_ref.shape[0]
    c = RW_CHUNK
    hd = HEAD_DIM
    nh = LANES // hd
    ri = lax.broadcasted_iota(jnp.int32, (c, c), 0)
    ci = lax.broadcasted_iota(jnp.int32, (c, c), 1)
    tril_incl = ri >= ci
    tril_strict = ri > ci
    is_diag = ri == ci
    ones_incl = jnp.where(tril_incl, 1.0, 0.0).astype(BF16)
    eye = jnp.where(is_diag, 1.0, 0.0)

    nc = rt_rows // c
    rows = [slice(cb * c, (cb + 1) * c) for cb in range(nc)]
    lanes = [slice(hh * hd, (hh + 1) * hd) for hh in range(nh)]
    items = [(cb, hh) for cb in range(nc) for hh in range(nh)]

    def cumulative(lw):
        lw_hi = _b(lw)
        lw_r = lw - lw_hi.astype(F32)
        lw_mid = _b(lw_r)
        lw_lo = _b(lw_r - lw_mid.astype(F32))
        g3 = _dot(ones_incl, jnp.concatenate([lw_hi, lw_mid, lw_lo], axis=1))
        return g3[:, :LANES] + g3[:, LANES:2 * LANES] + g3[:, 2 * LANES:]

    lw = [lw_ref[rs, :] for rs in rows]
    gcum = [cumulative(x) for x in lw]
    g_last = [x[c - 1:c] for x in gcum]
    kk = [kk_ref[rs, :] for rs in rows]
    k = [k_ref[rs, :] for rs in rows]
    v2 = [v_ref[rs, :] for rs in rows]
    b = [kk[cb] * a_ref[rows[cb], :] for cb in range(nc)]
    e_neg = [jnp.exp(-x) for x in gcum]
    e_end = [jnp.exp(g_last[cb] - gcum[cb]) for cb in range(nc)]
    a_t2 = [(-kk[cb]) * jnp.exp(gcum[cb] - lw[cb]) for cb in range(nc)]
    r_t2 = [r_ref[rows[cb], :] * jnp.exp(gcum[cb]) for cb in range(nc)]
    b_t2 = [b[cb] * e_neg[cb] for cb in range(nc)]
    k_t2 = [k[cb] * e_neg[cb] for cb in range(nc)]
    b_end2 = [b[cb] * e_end[cb] for cb in range(nc)]
    k_end2 = [k[cb] * e_end[cb] for cb in range(nc)]
    decay2 = [jnp.exp(x) for x in g_last]

    gram = [_dot_nt(_b(jnp.concatenate([a_t2[cb][:, lanes[hh]], r_t2[cb][:, lanes[hh]]], axis=0)),
                    _b(jnp.concatenate([b_t2[cb][:, lanes[hh]], k_t2[cb][:, lanes[hh]]], axis=0)))
            for cb, hh in items]
    l_ab = [jnp.where(tril_strict, x[:c, :c], 0.0) for x in gram]
    m_rb = [_b(jnp.where(tril_incl, x[c:, :c], 0.0)) for x in gram]
    lm = [_b(jnp.concatenate([jnp.where(tril_strict, x[:c, c:], 0.0),
                              jnp.where(tril_incl, x[c:, c:], 0.0)], axis=0)) for x in gram]
    v_b = [_b(v2[cb][:, lanes[hh]]) for cb, hh in items]
    lmv = [_dot(x, y) for x, y in zip(lm, v_b)]
    tinv = _tri_inverse(l_ab, eye)
    tw = [_dot(_b(tinv[i]), _b(jnp.concatenate([a_t2[cb][:, lanes[hh]], lmv[i][:c]], axis=1)))
          for i, (cb, hh) in enumerate(items)]
    tw_b = [_b(x) for x in tw]
    mb = [_dot(x, y) for x, y in zip(m_rb, tw_b)]
    q_eff = [_b(r_t2[cb][:, lanes[hh]] + mb[i][:, :hd]) for i, (cb, hh) in enumerate(items)]
    y_loc = [mb[i][:, hd:] + lmv[i][c:] for i in range(len(items))]
    a_c = [_b(jnp.where(is_diag, decay2[cb][:, lanes[hh]], 0.0)
              + _dot_tn(tw_b[i][:, :hd], _b(b_end2[cb][:, lanes[hh]])))
           for i, (cb, hh) in enumerate(items)]
    b_c = [_dot_tn(jnp.concatenate([tw_b[i][:, hd:], v_b[i]], axis=0),
                   _b(jnp.concatenate([b_end2[cb][:, lanes[hh]], k_end2[cb][:, lanes[hh]]], axis=0)))
           for i, (cb, hh) in enumerate(items)]

    states = [state_ref[hh] for hh in range(nh)]
    for i, (cb, hh) in enumerate(items):
        s0 = _b(states[hh])
        y = _dot_nt(q_eff[i], s0) + y_loc[i]
        states[hh] = _dot(s0, a_c[i]) + b_c[i]
        rs, ls = rows[cb], lanes[hh]
        mean = jnp.mean(y, axis=-1, keepdims=True)
        yc = y - mean
        var = jnp.mean(yc * yc, axis=-1, keepdims=True)
        yn = yc * lax.rsqrt(var + LNX_EPS) * lnw_ref[:, ls] + lnb_ref[:, ls]
        o_ref[rs, ls] = (yn + bonus_ref[rs, ls]) * g_ref[rs, ls]
    for hh in range(nh):
        state_ref[hh] = states[hh]


def _rwkv_chunk(r, lw, k, v, kk, a, g, bonus, lnw, lnb, batch, seq, rt=1024):
    m = r.shape[0]
    nt = seq // rt
    pairs = RW_W // LANES
    row = pl.BlockSpec((rt, LANES), lambda b, p, t: (b * nt + t, p))
    vec = pl.BlockSpec((1, LANES), lambda b, p, t: (0, p))
    return pl.pallas_call(
        _rwkv_chunk_kernel,
        grid=(batch, pairs, nt),
        in_specs=[row] * 8 + [vec, vec],
        out_specs=row,
        out_shape=jax.ShapeDtypeStruct((m, RW_W), F32),
        scratch_shapes=[pltpu.VMEM((LANES // HEAD_DIM, HEAD_DIM, HEAD_DIM), F32)],
        compiler_params=_cparams("parallel", "parallel", "arbitrary"),
        name="rwkv_chunk",
    )(r, lw, k, v, kk, a, g, bonus, lnw, lnb)


def _merge_kernel(x_ref, on_ref, or_ref, mg_ref, pa_ref, pb_ref, wo_ref, o_ref):
    d = x_ref.shape[1]
    ya = _dot(on_ref[...].astype(BF16), pa_ref[...])
    yb = _dot(or_ref[...].astype(BF16), pb_ref[...])
    y = _sigmoid(mg_ref[:, :d]) * ya + _sigmoid(mg_ref[:, d:]) * yb
    o_ref[...] = x_ref[...] + _dot(y.astype(BF16), wo_ref[...])


def _merge(x, o_nsa, o_rw, u_mg, pa, pb, wo, tm=512):
    m, d = x.shape
    full = lambda i: (0, 0)
    row = lambda i: (i, 0)
    return pl.pallas_call(
        _merge_kernel,
        grid=(m // tm,),
        in_specs=[pl.BlockSpec((tm, d), row), pl.BlockSpec((tm, NSA_W), row),
                  pl.BlockSpec((tm, RW_W), row), pl.BlockSpec((tm, 2 * d), row),
                  pl.BlockSpec((NSA_W, d), full), pl.BlockSpec((RW_W, d), full),
                  pl.BlockSpec((d, d), full)],
        out_specs=pl.BlockSpec((tm, d), row),
        out_shape=jax.ShapeDtypeStruct((m, d), F32),
        compiler_params=_cparams("parallel"),
        name="merge_out",
    )(x, o_nsa, o_rw, u_mg, pa, pb, wo)


FFN_HALO = 16


def _ffn_kernel(x_ref, xp_ref, g_ref, wa_ref, wb_ref, ca_ref, cb_ref, wd_ref, o_ref, h_scr, *,
                tiles_per_seq):
    i = pl.program_id(0)
    j = pl.program_id(1)

    @pl.when(j == 0)
    def _():
        x = x_ref[...]
        h_scr[FFN_HALO:, :] = _rms(x, g_ref[...]).astype(BF16)
        hp = _rms(xp_ref[...], g_ref[...])
        h_scr[0:FFN_HALO, :] = jnp.where(i % tiles_per_seq == 0, 0.0, hp).astype(BF16)
        o_ref[...] = x

    h = h_scr[...]

    def conv(u, c_ref):
        cw = c_ref[...]
        return (cw[2:3] * u[FFN_HALO:] + cw[1:2] * pltpu.roll(u, 1, axis=0)[FFN_HALO:]
                + cw[0:1] * pltpu.roll(u, 2, axis=0)[FFN_HALO:])

    a = conv(_dot(h, wa_ref[...]), ca_ref)
    b = conv(_dot(h, wb_ref[...]), cb_ref)
    o_ref[...] += _dot((_silu(a) * b).astype(BF16), wd_ref[...])


def _ffn(x, gain, w_up, conv_w, w_down, seq, tm=512, tf=D_FF):
    m, d = x.shape
    nf = D_FF // tf
    wmode = dict(pipeline_mode=pl.Buffered(1)) if nf == 1 else {}
    return pl.pallas_call(
        functools.partial(_ffn_kernel, tiles_per_seq=seq // tm),
        grid=(m // tm, nf),
        in_specs=[pl.BlockSpec((tm, d), lambda i, j: (i, 0)),
                  pl.BlockSpec((FFN_HALO, d),
                               lambda i, j: (jnp.maximum(i * (tm // FFN_HALO) - 1, 0), 0)),
                  pl.BlockSpec((1, d), lambda i, j: (0, 0)),
                  pl.BlockSpec((d, tf), lambda i, j: (0, j), **wmode),
                  pl.BlockSpec((d, tf), lambda i, j: (0, j + nf), **wmode),
                  pl.BlockSpec((3, tf), lambda i, j: (0, j)),
                  pl.BlockSpec((3, tf), lambda i, j: (0, j + nf)),
                  pl.BlockSpec((tf, d), lambda i, j: (j, 0), **wmode)],
        out_specs=pl.BlockSpec((tm, d), lambda i, j: (i, 0)),
        out_shape=jax.ShapeDtypeStruct((m, d), F32),
        scratch_shapes=[pltpu.VMEM((tm + FFN_HALO, d), BF16)],
        compiler_params=_cparams("parallel", "arbitrary"),
        name="conv_ffn",
    )(x, x, gain, w_up, w_up, conv_w, conv_w, w_down)


def _split_w_in(w):
    o = 0
    q = w[:, o:o + NSA_W]; o += NSA_W
    kv = w[:, o:o + 6 * KV_W]; o += 6 * KV_W
    n_gate = NSA_HEADS * 3
    gl = w[:, o:o + n_gate]; o += n_gate
    rw = w[:, o:o + RW_COLS]; o += RW_COLS
    mg = w[:, o:]
    per_g = n_gate // NSA_KV_HEADS
    pad = jnp.zeros((w.shape[0], LANES - per_g), w.dtype)
    gates = [jnp.concatenate([gl[:, g * per_g:(g + 1) * per_g], pad], axis=1)
             for g in range(NSA_KV_HEADS)]
    w_nsa = jnp.concatenate([q, kv] + gates, axis=1)
    return w_nsa.astype(BF16), rw.astype(BF16), mg.astype(BF16)


def kernel(x, norm_mix, norm_ffn, w_in, qk_gain, cmp_pe, cmp_w1, cmp_w2, rwkv_mu, rwkv_w0, rwkv_w_up,
           rwkv_a0, rwkv_a_up, rwkv_g_up, rwkv_k_k, rwkv_k_a, rwkv_r_k, rwkv_ln_w, rwkv_ln_b, vres_v0,
           vres_v1, vres_v2, proj_nsa, proj_rwkv, w_out, ffn_up, ffn_conv, ffn_down):
    batch, seq, d = x.shape
    depth = w_in.shape[0]
    m = batch * seq
    assert seq % SLC_KV_TILE == 0 and seq // SLC_BLOCK <= LANES and seq >= WINDOW + Q_TILE
    xf = x.reshape(m, d)
    half = CMP_STRIDE * HEAD_DIM
    hi = lax.broadcasted_iota(jnp.int32, (RW_W, RW_W), 0) // HEAD_DIM
    hj = lax.broadcasted_iota(jnp.int32, (RW_W, RW_W), 1) // HEAD_DIM
    seg = (hi == hj).astype(BF16)
    v_first = None
    for l in range(depth):
        w_nsa, w_rw, w_mg = _split_w_in(w_in[l])
        u_nsa, u_rw, u_mg = _inproj(xf, norm_mix[l][None], w_nsa, w_rw, w_mg)

        qn, cmp_in, ksa, vs, kwn, vw = _nsa_prep(u_nsa, qk_gain[l], seq)
        z = cmp_in.reshape(2, NSA_KV_HEADS, batch, seq // CMP_STRIDE, half)
        w1 = cmp_w1[l]
        w1ab = jnp.concatenate([w1[:, :CMP_STRIDE].reshape(2, half, HEAD_DIM),
                                w1[:, CMP_STRIDE:].reshape(2, half, HEAD_DIM)], axis=2).astype(BF16)
        w1flat = w1.reshape(2, 2 * half, HEAD_DIM).astype(BF16)
        pe8 = jnp.broadcast_to(cmp_pe[l].reshape(2, 1, 2 * half), (2, 8, 2 * half))
        cmp_kv = _nsa_compress(z, w1ab, pe8, w1flat, cmp_w2[l].astype(BF16), qk_gain[l])
        o_nsa = _nsa_attn(qn, u_nsa, cmp_kv, ksa, vs, kwn, vw, batch, seq)

        zero = jnp.zeros((RW_W,), F32)
        vec = jnp.stack([rwkv_w0[l], rwkv_a0[l], rwkv_k_k[l], rwkv_k_a[l], rwkv_r_k[l].reshape(RW_W),
                         zero, zero, zero])
        vres = None
        if l > 0:
            vres = (v_first, vres_v0[l - 1][None], vres_v1[l - 1].astype(BF16),
                    vres_v2[l - 1].astype(BF16))
        r, lw, kh, v, kk, a, g, bonus = _rwkv_prep(
            u_rw, rwkv_mu[l][None], vec, rwkv_w_up[l].astype(BF16), rwkv_a_up[l].astype(BF16),
            rwkv_g_up[l].astype(BF16), seg, vres, seq)
        if l == 0:
            v_first = v
        o_rw = _rwkv_chunk(r, lw, kh, v, kk, a, g, bonus, rwkv_ln_w[l][None], rwkv_ln_b[l][None],
                           batch, seq)

        xf = _merge(xf, o_nsa, o_rw, u_mg, proj_nsa[l].astype(BF16), proj_rwkv[l].astype(BF16),
                    w_out[l].astype(BF16))
        xf = _ffn(xf, norm_ffn[l][None], ffn_up[l].astype(BF16), ffn_conv[l],
                  ffn_down[l].astype(BF16), seq)
    return xf.reshape(batch, seq, d)
```

```python
import functools

import jax
import jax.numpy as jnp
from jax import lax
from jax.experimental import pallas as pl
from jax.experimental.pallas import tpu as pltpu

F32 = jnp.float32
BF16 = jnp.bfloat16

LANES = 128
VMEM_LIMIT_BYTES = 56 * 1024 * 1024

NSA_HEADS = 8
NSA_KV_HEADS = 2
NSA_GROUP = NSA_HEADS // NSA_KV_HEADS
HEAD_DIM = 64
CMP_BLOCK = 32
CMP_STRIDE = 16
SLC_BLOCK = 64
SLC_SHIFT = 6
SLC_TOPK = 16
N_FORCED = 3
WINDOW = 512
FORCE_SCORE = 1e4
NEG_INF = -1e30
LOG2_E = 1.4426950408889634
RWKV_HEADS = 8
DECAY_RANK = 64
ICLR_RANK = 64
GATE_RANK = 128
VRES_RANK = 32
LNX_EPS = 1e-5 * HEAD_DIM
D_FF = 2816
EPS = 1e-6

NSA_W = NSA_HEADS * HEAD_DIM
KV_W = NSA_KV_HEADS * HEAD_DIM
RW_W = RWKV_HEADS * HEAD_DIM
RW_COLS = 3 * RW_W + DECAY_RANK + ICLR_RANK + GATE_RANK
NSA_SLAB = NSA_W + 6 * KV_W + NSA_KV_HEADS * LANES

Q_TILE = 256
SLC_KV_TILE = 1024
RW_CHUNK = 64
RW_SUB = 16


def _cparams(*sem):
    return pltpu.CompilerParams(dimension_semantics=sem, vmem_limit_bytes=VMEM_LIMIT_BYTES)


def _dot(a, b):
    return lax.dot_general(a, b, (((1,), (0,)), ((), ())), preferred_element_type=F32)


def _dot_nt(a, b):
    return lax.dot_general(a, b, (((1,), (1,)), ((), ())), preferred_element_type=F32)


def _dot_tn(a, b):
    return lax.dot_general(a, b, (((0,), (0,)), ((), ())), preferred_element_type=F32)


def _rms(x, gain):
    return x * lax.rsqrt(jnp.mean(x * x, axis=-1, keepdims=True) + EPS) * gain


def _sigmoid(x):
    return 1.0 / (1.0 + jnp.exp(-x))


def _silu(x):
    return x * _sigmoid(x)


def _inproj_kernel(x_ref, g_ref, wn_ref, wr_ref, wm_ref, on_ref, or_ref, om_ref):
    h = _rms(x_ref[...], g_ref[...]).astype(BF16)
    on_ref[...] = _dot(h, wn_ref[...])
    or_ref[...] = _dot(h, wr_ref[...])
    om_ref[...] = _dot(h, wm_ref[...])


def _inproj(x, gain, w_nsa, w_rw, w_mg, tm=512):
    m, d = x.shape
    outs = [w_nsa.shape[1], w_rw.shape[1], w_mg.shape[1]]
    full = lambda i: (0, 0)
    return pl.pallas_call(
        _inproj_kernel,
        grid=(m // tm,),
        in_specs=[pl.BlockSpec((tm, d), lambda i: (i, 0)), pl.BlockSpec((1, d), full)]
        + [pl.BlockSpec((d, n), full, pipeline_mode=pl.Buffered(1)) for n in outs],
        out_specs=[pl.BlockSpec((tm, n), lambda i: (i, 0)) for n in outs],
        out_shape=[jax.ShapeDtypeStruct((m, n), F32) for n in outs],
        compiler_params=_cparams("parallel"),
        name="inproj",
    )(x, gain, w_nsa, w_rw, w_mg)


MASK_BIG = 2.0 ** 99
KS_AUG_W = LANES + HEAD_DIM


V_AUG_W = 2 * HEAD_DIM


def _nsa_prep_kernel(q_ref, kc_ref, vc_ref, ks_ref, vs_ref, kw_ref, vw_ref, gn_ref, qn_ref, cmp_ref,
                     ksa_ref, vs_o_ref, kwn_ref, vw_o_ref, *, tiles_per_seq):
    gn = gn_ref[...]
    tm = ks_ref.shape[0]
    pos = (pl.program_id(0) % tiles_per_seq) * tm + lax.broadcasted_iota(jnp.int32, (tm, LANES), 0)
    blk_col = lax.broadcasted_iota(jnp.int32, (tm, LANES), 1)
    onehot = jnp.where(jnp.right_shift(pos, SLC_SHIFT) == blk_col, -MASK_BIG, 0.0).astype(BF16)
    ones = jnp.ones((tm, HEAD_DIM), BF16)
    q_scale = HEAD_DIM ** -0.5 * LOG2_E
    for h in range(NSA_HEADS):
        qh = q_ref[:, h * HEAD_DIM:(h + 1) * HEAD_DIM]
        qn_ref[h // NSA_GROUP, h % NSA_GROUP] = (_rms(qh, gn[0:1]) * q_scale).astype(BF16)
    for g in range(NSA_KV_HEADS):
        sl = slice(g * HEAD_DIM, (g + 1) * HEAD_DIM)
        cmp_ref[0, g] = kc_ref[:, sl]
        cmp_ref[1, g] = vc_ref[:, sl]
        ksa_ref[g, :, :LANES] = onehot
        ksa_ref[g, :, LANES:] = _rms(ks_ref[:, sl], gn[2:3]).astype(BF16)
        kwn_ref[g] = _rms(kw_ref[:, sl], gn[3:4]).astype(BF16)
        vs_o_ref[g, :, :HEAD_DIM] = vs_ref[:, sl].astype(BF16)
        vs_o_ref[g, :, HEAD_DIM:] = ones
        vw_o_ref[g, :, :HEAD_DIM] = vw_ref[:, sl].astype(BF16)
        vw_o_ref[g, :, HEAD_DIM:] = ones


def _nsa_prep(u_nsa, qk_gain, seq, tm=512):
    m = u_nsa.shape[0]
    g = NSA_KV_HEADS
    col0 = NSA_W // KV_W
    in_specs = [pl.BlockSpec((tm, NSA_W), lambda i: (i, 0))]
    in_specs += [pl.BlockSpec((tm, KV_W), functools.partial(lambda i, c: (i, c), c=col0 + c))
                 for c in range(6)]
    in_specs.append(pl.BlockSpec((4, HEAD_DIM), lambda i: (0, 0)))
    spec3 = lambda w: pl.BlockSpec((g, tm, w), lambda i: (0, i, 0))
    shape3 = lambda w: jax.ShapeDtypeStruct((g, m, w), BF16)
    return pl.pallas_call(
        functools.partial(_nsa_prep_kernel, tiles_per_seq=seq // tm),
        grid=(m // tm,),
        in_specs=in_specs,
        out_specs=[pl.BlockSpec((g, NSA_GROUP, tm, HEAD_DIM), lambda i: (0, 0, i, 0)),
                   pl.BlockSpec((2, g, tm, HEAD_DIM), lambda i: (0, 0, i, 0)),
                   spec3(KS_AUG_W), spec3(V_AUG_W), spec3(HEAD_DIM), spec3(V_AUG_W)],
        out_shape=[jax.ShapeDtypeStruct((g, NSA_GROUP, m, HEAD_DIM), BF16),
                   jax.ShapeDtypeStruct((2, g, m, HEAD_DIM), F32),
                   shape3(KS_AUG_W), shape3(V_AUG_W), shape3(HEAD_DIM), shape3(V_AUG_W)],
        compiler_params=_cparams("parallel"),
        name="nsa_prep",
    )(*([u_nsa] * 7), qk_gain)


def _nsa_compress_kernel(z_ref, w1ab_ref, pe_ref, w1_ref, w2_ref, gn_ref, o_ref):
    kv = pl.program_id(0)
    nhalf = z_ref.shape[0]
    p = _dot(z_ref[...].astype(BF16), w1ab_ref[...])
    a = p[:, :HEAD_DIM]
    b_next = pltpu.roll(p[:, HEAD_DIM:], nhalf - 1, axis=0)
    c = _dot(pe_ref[...].astype(BF16), w1_ref[...])[0:1]
    hid = _silu(a + b_next + c)
    o = _dot(hid.astype(BF16), w2_ref[...])
    o_ref[:, :HEAD_DIM] = jnp.where(kv == 0, _rms(o, gn_ref[1:2]), o).astype(BF16)
    o_ref[:, HEAD_DIM:] = jnp.ones((nhalf, HEAD_DIM), BF16)


def _nsa_compress(z, w1ab, pe8, w1flat, w2, qk_gain):
    _, g, b, nhalf, zw = z.shape
    sq = lambda *shape: pl.BlockSpec((None,) + shape, lambda kv, gi, bi: (kv,) + (0,) * len(shape))
    return pl.pallas_call(
        _nsa_compress_kernel,
        grid=(2, g, b),
        in_specs=[pl.BlockSpec((None, None, None, nhalf, zw), lambda kv, gi, bi: (kv, gi, bi, 0, 0)),
                  sq(zw, 2 * HEAD_DIM), sq(8, 2 * zw), sq(2 * zw, HEAD_DIM), sq(HEAD_DIM, HEAD_DIM),
                  pl.BlockSpec((4, HEAD_DIM), lambda kv, gi, bi: (0, 0))],
        out_specs=pl.BlockSpec((None, None, None, nhalf, V_AUG_W),
                               lambda kv, gi, bi: (kv, gi, bi, 0, 0)),
        out_shape=jax.ShapeDtypeStruct((2, g, b, nhalf, V_AUG_W), BF16),
        compiler_params=_cparams("parallel", "parallel", "parallel"),
        name="nsa_compress",
    )(z, w1ab, pe8, w1flat, w2, qk_gain)


def _masked_attend(s, mask, v_aug):
    s = jnp.where(mask, s, NEG_INF)
    mx = jnp.max(s, axis=-1, keepdims=True)
    e = jnp.exp2(s - mx)
    ov = _dot(e.astype(BF16), v_aug)
    inv = jnp.where(mx > 0.5 * NEG_INF, 1.0 / ov[:, HEAD_DIM:HEAD_DIM + 1], 0.0)
    return e, ov[:, :HEAD_DIM] * inv, inv


def _nsa_attn_kernel(q_ref, gl_ref, kc_ref, vc_ref, ks_ref, vs_ref, kw_ref, vw_ref, o_ref):
    i = pl.program_id(2)
    hg, tq = q_ref.shape[0], q_ref.shape[1]
    t0 = i * tq
    q = q_ref[...].reshape(hg * tq, HEAD_DIM)
    t_tok = t0 + lax.broadcasted_iota(jnp.int32, (tq, 1), 0)
    t_rows = jnp.concatenate([t_tok] * hg, axis=0)

    ncp = kc_ref.shape[0]
    s = _dot_nt(q, kc_ref[:, :HEAD_DIM])
    n_idx = lax.broadcasted_iota(jnp.int32, (1, ncp), 1)
    cmp_valid = (n_idx * CMP_STRIDE + (CMP_BLOCK - 1)) <= t_rows
    e_cmp, o_cmp, inv_cmp = _masked_attend(s, cmp_valid, vc_ref[...])

    nsp = LANES
    p_sum = e_cmp[0:tq] * inv_cmp[0:tq]
    for h in range(1, hg):
        p_sum = p_sum + e_cmp[h * tq:(h + 1) * tq] * inv_cmp[h * tq:(h + 1) * tq]
    ov_n = lax.broadcasted_iota(jnp.int32, (ncp, nsp), 0) * CMP_STRIDE
    ov_s = lax.broadcasted_iota(jnp.int32, (ncp, nsp), 1) * SLC_BLOCK
    overlap = jnp.where((ov_n < ov_s + SLC_BLOCK) & (ov_n + CMP_BLOCK > ov_s), 1.0, 0.0).astype(BF16)
    p_hi = p_sum.astype(BF16)
    r1 = p_sum - p_hi.astype(F32)
    p_mid = r1.astype(BF16)
    p_lo = (r1 - p_mid.astype(F32)).astype(BF16)
    imp3 = _dot(jnp.concatenate([p_hi, p_mid, p_lo], axis=0), overlap)
    imp = imp3[0:tq] + imp3[tq:2 * tq] + imp3[2 * tq:3 * tq]
    s_idx = lax.broadcasted_iota(jnp.int32, (1, nsp), 1)
    cur = jnp.right_shift(t_tok, SLC_SHIFT)
    forced = (s_idx == 0) | (s_idx == cur) | (s_idx == cur - 1)
    imp = jnp.where(forced, FORCE_SCORE, imp)
    imp = jnp.where(s_idx > cur, -1.0, imp)

    work = imp.T
    blk = lax.broadcasted_iota(jnp.int32, (nsp, tq), 0).astype(F32)
    forced_t = work >= FORCE_SCORE
    unsel_t = jnp.where(forced_t, 0.0, 1.0)
    work = jnp.where(forced_t, -3e38, work)
    for _ in range(SLC_TOPK - N_FORCED):
        mx = jnp.max(work, axis=0, keepdims=True)
        first = jnp.min(jnp.where(work == mx, blk, float(nsp)), axis=0, keepdims=True)
        chosen = blk == first
        unsel_t = jnp.where(chosen, 0.0, unsel_t)
        work = jnp.where(chosen, -3e38, work)
    unsel = unsel_t.T.astype(BF16)

    span = WINDOW + tq
    w0 = pl.multiple_of(jnp.maximum(t0 - WINDOW, 0), tq)
    kw = kw_ref[pl.ds(w0, span), :]
    vw = vw_ref[pl.ds(w0, span), :]
    sw = _dot_nt(q, kw)
    dist = t_rows - (w0 + lax.broadcasted_iota(jnp.int32, (1, span), 1))
    _, o_win, _ = _masked_attend(sw, (dist >= 0) & (dist < WINDOW), vw)

    tk = SLC_KV_TILE
    q_aug = jnp.concatenate([jnp.concatenate([unsel] * hg, axis=0), q], axis=1)

    def slc_tile(j, carry, causal):
        k0 = pl.multiple_of(j * tk, tk)
        sc = _dot_nt(q_aug, ks_ref[pl.ds(k0, tk), :])
        if causal:
            sc = jnp.where(k0 + lax.broadcasted_iota(jnp.int32, (1, tk), 1) <= t_rows, sc, NEG_INF)
        m_i, acc = carry
        m_new = jnp.maximum(m_i, jnp.max(sc, axis=-1, keepdims=True))
        p = jnp.exp2(sc - m_new)
        return m_new, jnp.exp2(m_i - m_new) * acc + _dot(p.astype(BF16), vs_ref[pl.ds(k0, tk), :])

    init = (jnp.full((hg * tq, 1), NEG_INF, F32), jnp.zeros((hg * tq, V_AUG_W), F32))
    j_diag = t0 // tk
    carry = lax.fori_loop(0, j_diag, lambda j, c: slc_tile(j, c, False), init)
    _, acc_s = slc_tile(j_diag, carry, True)
    o_slc = acc_s[:, :HEAD_DIM] / acc_s[:, HEAD_DIM:HEAD_DIM + 1]

    gates = _sigmoid(gl_ref[...])
    outs = []
    for h in range(hg):
        r = slice(h * tq, (h + 1) * tq)
        outs.append(gates[:, 3 * h:3 * h + 1] * o_cmp[r] + gates[:, 3 * h + 1:3 * h + 2] * o_slc[r]
                    + gates[:, 3 * h + 2:3 * h + 3] * o_win[r])
    o_ref[...] = jnp.concatenate(outs, axis=1)


def _nsa_attn(qn, u_nsa, cmp_kv, ksa, vs, kwn, vw, batch, seq):
    m = u_nsa.shape[0]
    g = NSA_KV_HEADS
    nq = seq // Q_TILE
    ncp = cmp_kv.shape[3]
    gw = NSA_GROUP * HEAD_DIM
    gate_col0 = (NSA_W + 6 * KV_W) // LANES
    cmp_spec = lambda kv: pl.BlockSpec((None, None, None, ncp, V_AUG_W),
                                       lambda b, gi, i: (kv, gi, b, 0, 0))
    seq_spec = lambda w: pl.BlockSpec((None, seq, w), lambda b, gi, i: (gi, b, 0))
    return pl.pallas_call(
        _nsa_attn_kernel,
        grid=(batch, g, nq),
        in_specs=[pl.BlockSpec((None, NSA_GROUP, Q_TILE, HEAD_DIM),
                               lambda b, gi, i: (gi, 0, b * nq + i, 0)),
                  pl.BlockSpec((Q_TILE, LANES), lambda b, gi, i: (b * nq + i, gate_col0 + gi)),
                  cmp_spec(0), cmp_spec(1), seq_spec(KS_AUG_W), seq_spec(V_AUG_W),
                  seq_spec(HEAD_DIM), seq_spec(V_AUG_W)],
        out_specs=pl.BlockSpec((Q_TILE, gw), lambda b, gi, i: (b * nq + i, gi)),
        out_shape=jax.ShapeDtypeStruct((m, NSA_W), F32),
        compiler_params=_cparams("parallel", "parallel", "arbitrary"),
        name="nsa_attn",
    )(qn, u_nsa, cmp_kv, cmp_kv, ksa, vs, kwn, vw)


def _rwkv_prep_kernel(*refs, tiles_per_seq, has_vres):
    if has_vres:
        (z_ref, zp_ref, mu_ref, vec_ref, wup_ref, aup_ref, gup_ref, seg_ref, vf_ref, v0_ref, v1_ref,
         v2_ref, r_o, lw_o, k_o, v_o, kk_o, a_o, g_o, bonus_o) = refs
    else:
        (z_ref, zp_ref, mu_ref, vec_ref, wup_ref, aup_ref, gup_ref, seg_ref,
         r_o, lw_o, k_o, v_o, kk_o, a_o, g_o, bonus_o) = refs
    i = pl.program_id(0)
    z = z_ref[...]
    tm = z.shape[0]
    prev_last = jnp.where(i % tiles_per_seq == 0, 0.0, zp_ref[7:8, :])
    row = lax.broadcasted_iota(jnp.int32, (tm, 1), 0)
    z_prev = jnp.where(row == 0, prev_last, pltpu.roll(z, 1, axis=0))
    z = z + mu_ref[...] * (z_prev - z)
    w = RW_W
    r = z[:, 0:w]
    k = z[:, w:2 * w]
    v = z[:, 2 * w:3 * w]
    wd = z[:, 3 * w:3 * w + DECAY_RANK]
    ad = z[:, 3 * w + DECAY_RANK:3 * w + DECAY_RANK + ICLR_RANK]
    gd = z[:, 3 * w + DECAY_RANK + ICLR_RANK:]
    vec = vec_ref[...]
    w0, a0, k_k, k_a, r_k = (vec[n:n + 1] for n in range(5))
    if has_vres:
        lo = _dot(_dot(v.astype(BF16), v1_ref[...]).astype(BF16), v2_ref[...])
        v = v + (vf_ref[...] - v) * _sigmoid(v0_ref[...] + lo)
    wl = w0 + _dot(jnp.tanh(wd).astype(BF16), wup_ref[...])
    neg = -wl
    softplus = jnp.maximum(neg, 0.0) + jnp.log(1.0 + jnp.exp(-jnp.abs(neg)))
    lw_o[...] = -jnp.exp(-softplus - 0.5)
    a = _sigmoid(a0 + _dot(ad.astype(BF16), aup_ref[...]))
    g_o[...] = _dot(_sigmoid(gd).astype(BF16), gup_ref[...])
    seg = seg_ref[...]
    kk = k * k_k
    kh = k * (1.0 + (a - 1.0) * k_a)
    x = jnp.concatenate([kk * kk, r * kh * r_k], axis=0)
    x_hi = x.astype(BF16)
    x_r = x - x_hi.astype(F32)
    x_mid = x_r.astype(BF16)
    x_lo = (x_r - x_mid.astype(F32)).astype(BF16)
    sums = _dot(jnp.concatenate([x_hi, x_mid, x_lo], axis=0), seg)
    sums = sums[0:2 * tm] + sums[2 * tm:4 * tm] + sums[4 * tm:6 * tm]
    kk_o[...] = kk / jnp.maximum(jnp.sqrt(sums[0:tm]), 1e-12)
    bonus_o[...] = sums[tm:2 * tm] * v
    r_o[...] = r
    k_o[...] = kh
    v_o[...] = v
    a_o[...] = a


def _rwkv_prep(u_rw, mu, vec, wup, aup, gup, seg, vres, seq, tm=256):
    m = u_rw.shape[0]
    full = lambda i: (0, 0)
    row = lambda i: (i, 0)
    in_specs = [pl.BlockSpec((tm, RW_COLS), row),
                pl.BlockSpec((8, RW_COLS), lambda i: (jnp.maximum(i * (tm // 8) - 1, 0), 0)),
                pl.BlockSpec((1, RW_COLS), full), pl.BlockSpec((8, RW_W), full),
                pl.BlockSpec((DECAY_RANK, RW_W), full), pl.BlockSpec((ICLR_RANK, RW_W), full),
                pl.BlockSpec((GATE_RANK, RW_W), full), pl.BlockSpec((RW_W, RW_W), full)]
    args = [u_rw, u_rw, mu, vec, wup, aup, gup, seg]
    if vres is not None:
        v_first, v0, v1, v2 = vres
        in_specs += [pl.BlockSpec((tm, RW_W), row), pl.BlockSpec((1, RW_W), full),
                     pl.BlockSpec((RW_W, VRES_RANK), full), pl.BlockSpec((VRES_RANK, RW_W), full)]
        args += [v_first, v0, v1, v2]
    return pl.pallas_call(
        functools.partial(_rwkv_prep_kernel, tiles_per_seq=seq // tm, has_vres=vres is not None),
        grid=(m // tm,),
        in_specs=in_specs,
        out_specs=[pl.BlockSpec((tm, RW_W), row)] * 8,
        out_shape=[jax.ShapeDtypeStruct((m, RW_W), F32)] * 8,
        compiler_params=_cparams("parallel"),
        name="rwkv_prep",
    )(*args)


def _b(x):
    return x.astype(BF16)


def _tri_inverse(l_list, eye):
    c = eye.shape[0]
    ri = lax.broadcasted_iota(jnp.int32, (c, c), 0)
    ci = lax.broadcasted_iota(jnp.int32, (c, c), 1)
    same_blk = (ri // RW_SUB) == (ci // RW_SUB)
    ld = [jnp.where(same_blk, l, 0.0) for l in l_list]
    lo = [_b(l - d0) for l, d0 in zip(l_list, ld)]
    d = [eye + x for x in ld]
    p = [_b(x) for x in ld]
    sq = 1
    while 2 * sq < RW_SUB:
        p = [_b(_dot(x, x)) for x in p]
        d = [x + _dot(_b(x), y) for x, y in zip(d, p)]
        sq *= 2
    d = [_b(x) for x in d]
    n = [_dot(x, y) for x, y in zip(d, lo)]
    x = [eye + y for y in n]
    n = [_b(y) for y in n]
    terms = 2
    while terms < c // RW_SUB:
        n = [_b(_dot(y, y)) for y in n]
        x = [z + _dot(_b(z), y) for z, y in zip(x, n)]
        terms *= 2
    return [_dot(_b(z), y) for z, y in zip(x, d)]


def _rwkv_chunk_kernel(r_ref, lw_ref, k_ref, v_ref, kk_ref, a_ref, g_ref, bonus_ref, lnw_ref,
                       lnb_ref, o_ref, state_ref):
    t = pl.program_id(2)

    @pl.when(t == 0)
    def _():
        state_ref[...] = jnp.zeros_like(state_ref)

    rt_rows = r_ref.shape[0]
    c = RW_CHUNK
    hd = HEAD_DIM
    nh = LANES // hd
    ri = lax.broadcasted_iota(jnp.int32, (c, c), 0)
    ci = lax.broadcasted_iota(jnp.int32, (c, c), 1)
    tril_incl = ri >= ci
    tril_strict = ri > ci
    is_diag = ri == ci
    ones_incl = jnp.where(tril_incl, 1.0, 0.0).astype(BF16)
    eye = jnp.where(is_diag, 1.0, 0.0)

    nc = rt_rows // c
    rows = [slice(cb * c, (cb + 1) * c) for cb in range(nc)]
    lanes = [slice(hh * hd, (hh + 1) * hd) for hh in range(nh)]
    items = [(cb, hh) for cb in range(nc) for hh in range(nh)]

    def cumulative(lw):
        lw_hi = _b(lw)
        lw_r = lw - lw_hi.astype(F32)
        lw_mid = _b(lw_r)
        lw_lo = _b(lw_r - lw_mid.astype(F32))
        g3 = _dot(ones_incl, jnp.concatenate([lw_hi, lw_mid, lw_lo], axis=1))
        return g3[:, :LANES] + g3[:, LANES:2 * LANES] + g3[:, 2 * LANES:]

    lw = [lw_ref[rs, :] for rs in rows]
    gcum = [cumulative(x) for x in lw]
    g_last = [x[c - 1:c] for x in gcum]
    kk = [kk_ref[rs, :] for rs in rows]
    k = [k_ref[rs, :] for rs in rows]
    v2 = [v_ref[rs, :] for rs in rows]
    b = [kk[cb] * a_ref[rows[cb], :] for cb in range(nc)]
    e_neg = [jnp.exp(-x) for x in gcum]
    e_end = [jnp.exp(g_last[cb] - gcum[cb]) for cb in range(nc)]
    a_t2 = [(-kk[cb]) * jnp.exp(gcum[cb] - lw[cb]) for cb in range(nc)]
    r_t2 = [r_ref[rows[cb], :] * jnp.exp(gcum[cb]) for cb in range(nc)]
    b_t2 = [b[cb] * e_neg[cb] for cb in range(nc)]
    k_t2 = [k[cb] * e_neg[cb] for cb in range(nc)]
    b_end2 = [b[cb] * e_end[cb] for cb in range(nc)]
    k_end2 = [k[cb] * e_end[cb] for cb in range(nc)]
    decay2 = [jnp.exp(x) for x in g_last]

    gram = [_dot_nt(_b(jnp.concatenate([a_t2[cb][:, lanes[hh]], r_t2[cb][:, lanes[hh]]], axis=0)),
                    _b(jnp.concatenate([b_t2[cb][:, lanes[hh]], k_t2[cb][:, lanes[hh]]], axis=0)))
            for cb, hh in items]
    l_ab = [jnp.where(tril_strict, x[:c, :c], 0.0) for x in gram]
    m_rb = [_b(jnp.where(tril_incl, x[c:, :c], 0.0)) for x in gram]
    lm = [_b(jnp.concatenate([jnp.where(tril_strict, x[:c, c:], 0.0),
                              jnp.where(tril_incl, x[c:, c:], 0.0)], axis=0)) for x in gram]
    v_b = [_b(v2[cb][:, lanes[hh]]) for cb, hh in items]
    lmv = [_dot(x, y) for x, y in zip(lm, v_b)]
    tinv = _tri_inverse(l_ab, eye)
    tw = [_dot(_b(tinv[i]), _b(jnp.concatenate([a_t2[cb][:, lanes[hh]], lmv[i][:c]], axis=1)))
          for i, (cb, hh) in enumerate(items)]
    tw_b = [_b(x) for x in tw]
    mb = [_dot(x, y) for x, y in zip(m_rb, tw_b)]
    q_eff = [_b(r_t2[cb][:, lanes[hh]] + mb[i][:, :hd]) for i, (cb, hh) in enumerate(items)]
    y_loc = [mb[i][:, hd:] + lmv[i][c:] for i in range(len(items))]
    a_c = [_b(jnp.where(is_diag, decay2[cb][:, lanes[hh]], 0.0)
              + _dot_tn(tw_b[i][:, :hd], _b(b_end2[cb][:, lanes[hh]])))
           for i, (cb, hh) in enumerate(items)]
    b_c = [_dot_tn(jnp.concatenate([tw_b[i][:, hd:], v_b[i]], axis=0),
                   _b(jnp.concatenate([b_end2[cb][:, lanes[hh]], k_end2[cb][:, lanes[hh]]], axis=0)))
           for i, (cb, hh) in enumerate(items)]

    states = [state_ref[hh] for hh in range(nh)]
    for i, (cb, hh) in enumerate(items):
        s0 = _b(states[hh])
        y = _dot_nt(q_eff[i], s0) + y_loc[i]
        states[hh] = _dot(s0, a_c[i]) + b_c[i]
        rs, ls = rows[cb], lanes[hh]
        mean = jnp.mean(y, axis=-1, keepdims=True)
        yc = y - mean
        var = jnp.mean(yc * yc, axis=-1, keepdims=True)
        yn = yc * lax.rsqrt(var + LNX_EPS) * lnw_ref[:, ls] + lnb_ref[:, ls]
        o_ref[rs, ls] = (yn + bonus_ref[rs, ls]) * g_ref[rs, ls]
    for hh in range(nh):
        state_ref[hh] = states[hh]


def _rwkv_chunk(r, lw, k, v, kk, a, g, bonus, lnw, lnb, batch, seq, rt=2048):
    m = r.shape[0]
    nt = seq // rt
    pairs = RW_W // LANES
    row = pl.BlockSpec((rt, LANES), lambda b, p, t: (b * nt + t, p))
    vec = pl.BlockSpec((1, LANES), lambda b, p, t: (0, p))
    return pl.pallas_call(
        _rwkv_chunk_kernel,
        grid=(batch, pairs, nt),
        in_specs=[row] * 8 + [vec, vec],
        out_specs=row,
        out_shape=jax.ShapeDtypeStruct((m, RW_W), F32),
        scratch_shapes=[pltpu.VMEM((LANES // HEAD_DIM, HEAD_DIM, HEAD_DIM), F32)],
        compiler_params=_cparams("parallel", "parallel", "arbitrary"),
        name="rwkv_chunk",
    )(r, lw, k, v, kk, a, g, bonus, lnw, lnb)


def _merge_kernel(x_ref, on_ref, or_ref, mg_ref, pa_ref, pb_ref, wo_ref, o_ref):
    d = x_ref.shape[1]
    ya = _dot(on_ref[...].astype(BF16), pa_ref[...])
    yb = _dot(or_ref[...].astype(BF16), pb_ref[...])
    y = _sigmoid(mg_ref[:, :d]) * ya + _sigmoid(mg_ref[:, d:]) * yb
    o_ref[...] = x_ref[...] + _dot(y.astype(BF16), wo_ref[...])


def _merge(x, o_nsa, o_rw, u_mg, pa, pb, wo, tm=512):
    m, d = x.shape
    full = lambda i: (0, 0)
    row = lambda i: (i, 0)
    return pl.pallas_call(
        _merge_kernel,
        grid=(m // tm,),
        in_specs=[pl.BlockSpec((tm, d), row), pl.BlockSpec((tm, NSA_W), row),
                  pl.BlockSpec((tm, RW_W), row), pl.BlockSpec((tm, 2 * d), row),
                  pl.BlockSpec((NSA_W, d), full), pl.BlockSpec((RW_W, d), full),
                  pl.BlockSpec((d, d), full)],
        out_specs=pl.BlockSpec((tm, d), row),
        out_shape=jax.ShapeDtypeStruct((m, d), F32),
        compiler_params=_cparams("parallel"),
        name="merge_out",
    )(x, o_nsa, o_rw, u_mg, pa, pb, wo)


FFN_HALO = 16


def _ffn_kernel(x_ref, xp_ref, g_ref, wa_ref, wb_ref, ca_ref, cb_ref, wd_ref, o_ref, h_scr, *,
                tiles_per_seq):
    i = pl.program_id(0)
    j = pl.program_id(1)

    @pl.when(j == 0)
    def _():
        x = x_ref[...]
        h_scr[FFN_HALO:, :] = _rms(x, g_ref[...]).astype(BF16)
        hp = _rms(xp_ref[...], g_ref[...])
        h_scr[0:FFN_HALO, :] = jnp.where(i % tiles_per_seq == 0, 0.0, hp).astype(BF16)
        o_ref[...] = x

    h = h_scr[...]

    def conv(u, c_ref):
        cw = c_ref[...]
        return (cw[2:3] * u[FFN_HALO:] + cw[1:2] * pltpu.roll(u, 1, axis=0)[FFN_HALO:]
                + cw[0:1] * pltpu.roll(u, 2, axis=0)[FFN_HALO:])

    a = conv(_dot(h, wa_ref[...]), ca_ref)
    b = conv(_dot(h, wb_ref[...]), cb_ref)
    o_ref[...] += _dot((_silu(a) * b).astype(BF16), wd_ref[...])


def _ffn(x, gain, w_up, conv_w, w_down, seq, tm=512, tf=D_FF):
    m, d = x.shape
    nf = D_FF // tf
    wmode = dict(pipeline_mode=pl.Buffered(1)) if nf == 1 else {}
    return pl.pallas_call(
        functools.partial(_ffn_kernel, tiles_per_seq=seq // tm),
        grid=(m // tm, nf),
        in_specs=[pl.BlockSpec((tm, d), lambda i, j: (i, 0)),
                  pl.BlockSpec((FFN_HALO, d),
                               lambda i, j: (jnp.maximum(i * (tm // FFN_HALO) - 1, 0), 0)),
                  pl.BlockSpec((1, d), lambda i, j: (0, 0)),
                  pl.BlockSpec((d, tf), lambda i, j: (0, j), **wmode),
                  pl.BlockSpec((d, tf), lambda i, j: (0, j + nf), **wmode),
                  pl.BlockSpec((3, tf), lambda i, j: (0, j)),
                  pl.BlockSpec((3, tf), lambda i, j: (0, j + nf)),
                  pl.BlockSpec((tf, d), lambda i, j: (j, 0), **wmode)],
        out_specs=pl.BlockSpec((tm, d), lambda i, j: (i, 0)),
        out_shape=jax.ShapeDtypeStruct((m, d), F32),
        scratch_shapes=[pltpu.VMEM((tm + FFN_HALO, d), BF16)],
        compiler_params=_cparams("parallel", "arbitrary"),
        name="conv_ffn",
    )(x, x, gain, w_up, w_up, conv_w, conv_w, w_down)


def _split_w_in(w):
    o = 0
    q = w[:, o:o + NSA_W]; o += NSA_W
    kv = w[:, o:o + 6 * KV_W]; o += 6 * KV_W
    n_gate = NSA_HEADS * 3
    gl = w[:, o:o + n_gate]; o += n_gate
    rw = w[:, o:o + RW_COLS]; o += RW_COLS
    mg = w[:, o:]
    per_g = n_gate // NSA_KV_HEADS
    pad = jnp.zeros((w.shape[0], LANES - per_g), w.dtype)
    gates = [jnp.concatenate([gl[:, g * per_g:(g + 1) * per_g], pad], axis=1)
             for g in range(NSA_KV_HEADS)]
    w_nsa = jnp.concatenate([q, kv] + gates, axis=1)
    return w_nsa.astype(BF16), rw.astype(BF16), mg.astype(BF16)


def kernel(x, norm_mix, norm_ffn, w_in, qk_gain, cmp_pe, cmp_w1, cmp_w2, rwkv_mu, rwkv_w0, rwkv_w_up,
           rwkv_a0, rwkv_a_up, rwkv_g_up, rwkv_k_k, rwkv_k_a, rwkv_r_k, rwkv_ln_w, rwkv_ln_b, vres_v0,
           vres_v1, vres_v2, proj_nsa, proj_rwkv, w_out, ffn_up, ffn_conv, ffn_down):
    batch, seq, d = x.shape
    depth = w_in.shape[0]
    m = batch * seq
    assert seq % SLC_KV_TILE == 0 and seq // SLC_BLOCK <= LANES and seq >= WINDOW + Q_TILE
    xf = x.reshape(m, d)
    half = CMP_STRIDE * HEAD_DIM
    hi = lax.broadcasted_iota(jnp.int32, (RW_W, RW_W), 0) // HEAD_DIM
    hj = lax.broadcasted_iota(jnp.int32, (RW_W, RW_W), 1) // HEAD_DIM
    seg = (hi == hj).astype(BF16)
    v_first = None
    for l in range(depth):
        w_nsa, w_rw, w_mg = _split_w_in(w_in[l])
        u_nsa, u_rw, u_mg = _inproj(xf, norm_mix[l][None], w_nsa, w_rw, w_mg)

        qn, cmp_in, ksa, vs, kwn, vw = _nsa_prep(u_nsa, qk_gain[l], seq)
        z = cmp_in.reshape(2, NSA_KV_HEADS, batch, seq // CMP_STRIDE, half)
        w1 = cmp_w1[l]
        w1ab = jnp.concatenate([w1[:, :CMP_STRIDE].reshape(2, half, HEAD_DIM),
                                w1[:, CMP_STRIDE:].reshape(2, half, HEAD_DIM)], axis=2).astype(BF16)
        w1flat = w1.reshape(2, 2 * half, HEAD_DIM).astype(BF16)
        pe8 = jnp.broadcast_to(cmp_pe[l].reshape(2, 1, 2 * half), (2, 8, 2 * half))
        cmp_kv = _nsa_compress(z, w1ab, pe8, w1flat, cmp_w2[l].astype(BF16), qk_gain[l])
        o_nsa = _nsa_attn(qn, u_nsa, cmp_kv, ksa, vs, kwn, vw, batch, seq)

        zero = jnp.zeros((RW_W,), F32)
        vec = jnp.stack([rwkv_w0[l], rwkv_a0[l], rwkv_k_k[l], rwkv_k_a[l], rwkv_r_k[l].reshape(RW_W),
                         zero, zero, zero])
        vres = None
        if l > 0:
            vres = (v_first, vres_v0[l - 1][None], vres_v1[l - 1].astype(BF16),
                    vres_v2[l - 1].astype(BF16))
        r, lw, kh, v, kk, a, g, bonus = _rwkv_prep(
            u_rw, rwkv_mu[l][None], vec, rwkv_w_up[l].astype(BF16), rwkv_a_up[l].astype(BF16),
            rwkv_g_up[l].astype(BF16), seg, vres, seq)
        if l == 0:
            v_first = v
        o_rw = _rwkv_chunk(r, lw, kh, v, kk, a, g, bonus, rwkv_ln_w[l][None], rwkv_ln_b[l][None],
                           batch, seq)

        xf = _merge(xf, o_nsa, o_rw, u_mg, proj_nsa[l].astype(BF16), proj_rwkv[l].astype(BF16),
                    w_out[l].astype(BF16))
        xf = _ffn(xf, norm_ffn[l][None], ffn_up[l].astype(BF16), ffn_conv[l],
                  ffn_down[l].astype(BF16), seq)
    return xf.reshape(batch, seq, d)
```
